```python
import jax, jax.numpy as jnp
from jax import lax
import numpy as np

D_MODEL = 1024
BATCH = 8
SEQ = 2048
DEPTH = 1
DEC_BATCH = 128
DEC_SEQ = 1
PAST_LEN = 16384
PAGE_SIZE = 128

D_CONV = D_MODEL // 2
D_RWKV = D_MODEL - D_CONV
D_MIX = D_CONV + D_RWKV
CONV_W = 31
HEAD_DIM = 64
N_HEADS = D_RWKV // HEAD_DIM
LORA_W = 64
LORA_A = 64
LORA_G = 160
D_SHIFT = 3 * D_RWKV + LORA_W + LORA_A + LORA_G
D_IN = 2 * D_CONV + D_SHIFT
V_COL = 2 * D_CONV + 2 * D_RWKV
GN_EPS = 64e-5
LN_EPS = 1e-5
N_EXPERTS = 64
D_EXPERT = 256
TOP_K = 8
N_GROUPS = 8
TOPK_GROUPS = 4
ROUTED_SCALE = 2.5
MOE_BLOCK = 128
ALPHA = (2.0 * DEPTH) ** 0.25
BETA = (8.0 * DEPTH) ** -0.25

kernel_name = "hymba_conformer_rwkv7_moe_deepnorm_step"


def layer_norm(x, g, b, eps=LN_EPS):
    xf = x.astype(jnp.float32)
    mu = xf.mean(-1, keepdims=True)
    var = jnp.square(xf - mu).mean(-1, keepdims=True)
    return ((xf - mu) * lax.rsqrt(var + eps) * g + b).astype(x.dtype)


def wkv_scan(s0, r, w, k, v, a, b):
    def step(s, inp):
        r_t, w_t, k_t, v_t, a_t, b_t = inp
        sa = jnp.einsum("bhvk,bhk->bhv", s, a_t)
        s = s * w_t[:, :, None, :] + sa[..., None] * b_t[:, :, None, :] + v_t[..., None] * k_t[:, :, None, :]
        return s, jnp.einsum("bhvk,bhk->bhv", s, r_t)
    xs = tuple(jnp.swapaxes(t, 0, 1) for t in (r, w, k, v, a, b))
    s_final, ys = lax.scan(step, s0, xs)
    return jnp.swapaxes(ys, 0, 1), s_final


def moe(x2d, router_w, router_bias, exp_gate, exp_up, exp_down, sh_gate, sh_up, sh_down):
    T = x2d.shape[0]
    scores = jax.nn.sigmoid(jnp.dot(x2d, router_w).astype(jnp.float32))
    sel = scores + router_bias.astype(jnp.float32)
    grp = sel.reshape(T, N_GROUPS, N_EXPERTS // N_GROUPS)
    grp_score = lax.top_k(grp, 2)[0].sum(-1)
    _, gidx = lax.top_k(grp_score, TOPK_GROUPS)
    gmask = jax.nn.one_hot(gidx, N_GROUPS, dtype=jnp.float32).sum(-2) > 0
    emask = jnp.repeat(gmask, N_EXPERTS // N_GROUPS, axis=-1)
    _, eidx = lax.top_k(jnp.where(emask, sel, -jnp.inf), TOP_K)
    gates = jnp.take_along_axis(scores, eidx, axis=-1)
    gates = gates / (gates.sum(-1, keepdims=True) + 1e-20) * ROUTED_SCALE

    S = T * TOP_K
    n_blocks = -(-(S + N_EXPERTS * (MOE_BLOCK - 1)) // MOE_BLOCK)
    P = n_blocks * MOE_BLOCK
    flat_e = eidx.reshape(S)
    flat_t = jnp.repeat(jnp.arange(T, dtype=jnp.int32), TOP_K)
    flat_g = gates.reshape(S).astype(x2d.dtype)
    order = jnp.argsort(flat_e)
    se, st, sg = flat_e[order], flat_t[order], flat_g[order]
    counts = jnp.bincount(flat_e, length=N_EXPERTS)
    starts = jnp.cumsum(counts) - counts
    pcounts = (counts + MOE_BLOCK - 1) // MOE_BLOCK * MOE_BLOCK
    pends = jnp.cumsum(pcounts)
    pstarts = pends - pcounts
    dest = pstarts[se] + (jnp.arange(S) - starts[se])
    buf_t = jnp.zeros((P,), jnp.int32).at[dest].set(st)
    buf_g = jnp.zeros((P,), x2d.dtype).at[dest].set(sg)
    block_e = jnp.minimum(jnp.searchsorted(pends, jnp.arange(n_blocks) * MOE_BLOCK, side="right"), N_EXPERTS - 1)

    def block_fn(args):
        t_idx, g, e = args
        xb = x2d[t_idx]
        hb = jax.nn.silu(xb @ exp_gate[e]) * (xb @ exp_up[e])
        return (hb @ exp_down[e]) * g[:, None]

    yb = lax.map(block_fn, (buf_t.reshape(n_blocks, MOE_BLOCK), buf_g.reshape(n_blocks, MOE_BLOCK), block_e))
    routed = jax.ops.segment_sum(yb.reshape(P, -1), buf_t, num_segments=T)
    shared = (jax.nn.silu(x2d @ sh_gate) * (x2d @ sh_up)) @ sh_down
    return shared + routed


def hybrid_layer(x, conv_buf, shift_buf, wkv_state, lp):
    Bn, T, _ = x.shape
    z = jnp.einsum("btd,de->bte", x, lp["w_in"])
    z_conv, z_rw = z[..., :2 * D_CONV], z[..., 2 * D_CONV:]

    u = z_conv[..., :D_CONV] * jax.nn.sigmoid(z_conv[..., D_CONV:])
    u_full = jnp.concatenate([conv_buf.astype(u.dtype), u], axis=1)
    c = lax.conv_general_dilated(u_full, lp["conv_w"][:, None, :], (1,), "VALID",
                                 dimension_numbers=("NWC", "WIO", "NWC"),
                                 feature_group_count=D_CONV) + lp["conv_b"]
    c = jax.nn.silu(layer_norm(c, lp["conv_ln_g"], lp["conv_ln_b"]))
    new_conv = u_full[:, -(CONV_W - 1):]

    z_prev = jnp.concatenate([shift_buf[:, None].astype(z_rw.dtype), z_rw[:, :-1]], axis=1)
    zs = z_rw + (z_prev - z_rw) * lp["mu_shift"]
    new_shift = z_rw[:, -1]
    r, k, v, wd, ad, gd = jnp.split(
        zs, [D_RWKV, 2 * D_RWKV, 3 * D_RWKV, 3 * D_RWKV + LORA_W, 3 * D_RWKV + LORA_W + LORA_A], axis=-1)
    w_log = -jax.nn.softplus(-(lp["w0"] + jnp.tanh(wd) @ lp["w_up"])) - 0.5
    decay = jnp.exp(-jnp.exp(w_log.astype(jnp.float32)))
    a = jax.nn.sigmoid(lp["a0"] + ad @ lp["a_up"])
    g = jax.nn.sigmoid(gd) @ lp["g_up"]

    def heads(t):
        return t.reshape(Bn, T, N_HEADS, HEAD_DIM).astype(jnp.float32)

    kk = heads(k * lp["k_k"])
    kk = kk / jnp.maximum(jnp.sqrt(jnp.sum(kk * kk, -1, keepdims=True)), 1e-12)
    k = k * (1.0 + (a - 1.0) * lp["k_a"])
    rh, kh, vh, ah = heads(r), heads(k), heads(v), heads(a)
    y, new_wkv = wkv_scan(wkv_state.astype(jnp.float32), rh, heads(decay), kh, vh, -kk, kk * ah)
    y_mu = y.mean(-1, keepdims=True)
    y_var = jnp.square(y - y_mu).mean(-1, keepdims=True)
    yn = ((y - y_mu) * lax.rsqrt(y_var + GN_EPS)).reshape(Bn, T, D_RWKV) * lp["gn_g"] + lp["gn_b"]
    bonus = (jnp.sum(rh * kh * lp["r_k"].astype(jnp.float32), -1, keepdims=True) * vh).reshape(Bn, T, D_RWKV)
    o_rw = ((yn + bonus) * g).astype(x.dtype)

    h = jnp.einsum("btc,cd->btd", jnp.concatenate([c.astype(x.dtype), o_rw], axis=-1), lp["w_out"])
    x = layer_norm(ALPHA * x + h, lp["ln1_g"], lp["ln1_b"])
    f = moe(x.reshape(Bn * T, D_MODEL), lp["router_w"], lp["router_bias"], lp["exp_gate"], lp["exp_up"],
            lp["exp_down"], lp["sh_gate"], lp["sh_up"], lp["sh_down"]).reshape(Bn, T, D_MODEL)
    x = layer_norm(ALPHA * x + f.astype(x.dtype), lp["ln2_g"], lp["ln2_b"])
    return x, new_conv, new_shift.astype(shift_buf.dtype), new_wkv.astype(wkv_state.dtype)


def setup_inputs(seed: int = 0) -> dict:
    key = jax.random.key(seed)
    ks = jax.random.split(key, 40)
    f32 = jnp.float32

    def nrm(k, shape, s):
        return jax.random.normal(k, shape, f32) * s

    col_scale = jnp.ones((D_IN,), f32).at[V_COL:V_COL + D_RWKV].set(BETA)
    return {
        "x_prompt": nrm(ks[0], (BATCH, SEQ, D_MODEL), 1.0),
        "x_sample": nrm(ks[1], (DEC_BATCH, DEC_SEQ, D_MODEL), 1.0),
        "state_conv": nrm(ks[2], (DEPTH, DEC_BATCH, CONV_W - 1, D_CONV), 0.5),
        "state_shift": nrm(ks[3], (DEPTH, DEC_BATCH, D_SHIFT), 1.0),
        "state_wkv": nrm(ks[4], (DEPTH, DEC_BATCH, N_HEADS, HEAD_DIM, HEAD_DIM), 0.3),
        "w_in": nrm(ks[5], (DEPTH, D_MODEL, D_IN), D_MODEL ** -0.5) * col_scale,
        "mu_shift": jax.random.uniform(ks[6], (DEPTH, D_SHIFT), f32),
        "conv_w": nrm(ks[7], (DEPTH, CONV_W, D_CONV), CONV_W ** -0.5),
        "conv_b": nrm(ks[8], (DEPTH, D_CONV), 0.02),
        "conv_ln_g": 1.0 + nrm(ks[9], (DEPTH, D_CONV), 0.02),
        "conv_ln_b": nrm(ks[10], (DEPTH, D_CONV), 0.02),
        "w0": jax.random.uniform(ks[11], (DEPTH, D_RWKV), f32, minval=-6.0, maxval=1.0),
        "w_up": nrm(ks[12], (DEPTH, LORA_W, D_RWKV), 0.1 * LORA_W ** -0.5),
        "a0": nrm(ks[13], (DEPTH, D_RWKV), 0.5),
        "a_up": nrm(ks[14], (DEPTH, LORA_A, D_RWKV), LORA_A ** -0.5),
        "g_up": nrm(ks[15], (DEPTH, LORA_G, D_RWKV), LORA_G ** -0.5),
        "k_k": 0.85 + nrm(ks[16], (DEPTH, D_RWKV), 0.05),
        "k_a": 1.0 + nrm(ks[17], (DEPTH, D_RWKV), 0.05),
        "r_k": nrm(ks[18], (DEPTH, N_HEADS, HEAD_DIM), 0.1),
        "gn_g": 1.0 + nrm(ks[19], (DEPTH, D_RWKV), 0.02),
        "gn_b": nrm(ks[20], (DEPTH, D_RWKV), 0.02),
        "w_out": nrm(ks[21], (DEPTH, D_MIX, D_MODEL), BETA * D_MIX ** -0.5),
        "ln1_g": 1.0 + nrm(ks[22], (DEPTH, D_MODEL), 0.02),
        "ln1_b": nrm(ks[23], (DEPTH, D_MODEL), 0.02),
        "router_w": nrm(ks[24], (DEPTH, D_MODEL, N_EXPERTS), D_MODEL ** -0.5),
        "router_bias": nrm(ks[25], (DEPTH, N_EXPERTS), 0.01),
        "exp_gate": nrm(ks[26], (DEPTH, N_EXPERTS, D_MODEL, D_EXPERT), D_MODEL ** -0.5),
        "exp_up": nrm(ks[27], (DEPTH, N_EXPERTS, D_MODEL, D_EXPERT), D_MODEL ** -0.5),
        "exp_down": nrm(ks[28], (DEPTH, N_EXPERTS, D_EXPERT, D_MODEL), BETA * D_EXPERT ** -0.5),
        "sh_gate": nrm(ks[29], (DEPTH, D_MODEL, D_EXPERT), D_MODEL ** -0.5),
        "sh_up": nrm(ks[30], (DEPTH, D_MODEL, D_EXPERT), D_MODEL ** -0.5),
        "sh_down": nrm(ks[31], (DEPTH, D_EXPERT, D_MODEL), BETA * D_EXPERT ** -0.5),
        "ln2_g": 1.0 + nrm(ks[32], (DEPTH, D_MODEL), 0.02),
        "ln2_b": nrm(ks[33], (DEPTH, D_MODEL), 0.02),
    }


def reference(x_prompt, x_sample, state_conv, state_shift, state_wkv, w_in, mu_shift, conv_w, conv_b,
              conv_ln_g, conv_ln_b, w0, w_up, a0, a_up, g_up, k_k, k_a, r_k, gn_g, gn_b, w_out,
              ln1_g, ln1_b, router_w, router_bias, exp_gate, exp_up, exp_down, sh_gate, sh_up, sh_down,
              ln2_g, ln2_b):
    y_prompt, y_sample = x_prompt, x_sample
    conv_p, shift_p, wkv_p, conv_s, shift_s, wkv_s = [], [], [], [], [], []
    Bp = x_prompt.shape[0]
    for l in range(DEPTH):
        lp = {"w_in": w_in[l], "mu_shift": mu_shift[l], "conv_w": conv_w[l], "conv_b": conv_b[l],
              "conv_ln_g": conv_ln_g[l], "conv_ln_b": conv_ln_b[l], "w0": w0[l], "w_up": w_up[l],
              "a0": a0[l], "a_up": a_up[l], "g_up": g_up[l], "k_k": k_k[l], "k_a": k_a[l], "r_k": r_k[l],
              "gn_g": gn_g[l], "gn_b": gn_b[l], "w_out": w_out[l], "ln1_g": ln1_g[l], "ln1_b": ln1_b[l],
              "router_w": router_w[l], "router_bias": router_bias[l], "exp_gate": exp_gate[l],
              "exp_up": exp_up[l], "exp_down": exp_down[l], "sh_gate": sh_gate[l], "sh_up": sh_up[l],
              "sh_down": sh_down[l], "ln2_g": ln2_g[l], "ln2_b": ln2_b[l]}
        zc = jnp.zeros((Bp, CONV_W - 1, D_CONV), x_prompt.dtype)
        zsh = jnp.zeros((Bp, D_SHIFT), x_prompt.dtype)
        zw = jnp.zeros((Bp, N_HEADS, HEAD_DIM, HEAD_DIM), x_prompt.dtype)
        y_prompt, c_p, s_p, w_p = hybrid_layer(y_prompt, zc, zsh, zw, lp)
        y_sample, c_s, s_s, w_s = hybrid_layer(y_sample, state_conv[l], state_shift[l], state_wkv[l], lp)
        conv_p.append(c_p); shift_p.append(s_p); wkv_p.append(w_p)
        conv_s.append(c_s); shift_s.append(s_s); wkv_s.append(w_s)
    new_conv_p = jnp.stack(conv_p)
    new_shift_p = jnp.stack(shift_p)
    new_wkv_p = jnp.stack(wkv_p)
    new_conv_s = jnp.stack(conv_s)
    new_shift_s = jnp.stack(shift_s)
    new_wkv_s = jnp.stack(wkv_s)
    return (y_prompt, y_sample, new_conv_p, new_shift_p, new_wkv_p, new_conv_s, new_shift_s, new_wkv_s)
```

```python
import functools

import jax
import jax.numpy as jnp
import numpy as np
from jax import lax
from jax.experimental import pallas as pl
from jax.experimental.pallas import tpu as pltpu

F32 = jnp.float32
BF16 = jnp.bfloat16

HEAD_DIM = 64
N_SEQ = 8
GROUP = 16
ROWS = GROUP * N_SEQ
LANES = 128
LN_EPS = 1e-5
GN_EPS = 64e-5
ROUTED_SCALE = 2.5
N_GROUPS = 8
TOPK_GROUPS = 4
TOP_K = 8
SEQ_TILE_ROWS = 512
WKV_TOKENS = 64
MOE_ROWS = 1376
VMEM_LIMIT = 48 * 1024 * 1024


def _params(*sem):
    return pltpu.CompilerParams(dimension_semantics=sem, vmem_limit_bytes=VMEM_LIMIT)


def _full(shape):
    n = len(shape)
    return pl.BlockSpec(shape, lambda *_: (0,) * n)


def _dot(a, b):
    return jnp.dot(a, b, preferred_element_type=F32)


def _dot_nt(a, b):
    return lax.dot_general(a, b, (((1,), (1,)), ((), ())), preferred_element_type=F32)


def _split_dot(x, m):
    hi = x.astype(BF16)
    lo = (x - hi.astype(F32)).astype(BF16)
    return _dot(hi, m) + _dot(lo, m)


def _layer_norm(x, g, b):
    mu = jnp.mean(x, axis=-1, keepdims=True)
    d = x - mu
    var = jnp.mean(d * d, axis=-1, keepdims=True)
    return d * lax.rsqrt(var + LN_EPS) * g + b


def _silu(x):
    return x * jax.nn.sigmoid(x)


def _in_proj_kernel(xp_ref, xs_ref, pt_ref, wcv_ref, wrkv_ref, wlo_ref, u_ref, zr_ref, zl_ref):
    i = pl.program_id(0)
    last = pl.num_programs(0) - 1
    xb = xp_ref[...].reshape(ROWS, xp_ref.shape[-1]).astype(BF16)
    xt = _dot(pt_ref[...], xb).astype(BF16)
    x = jnp.where(i == last, xs_ref[...].astype(BF16), xt)
    zc = _dot(x, wcv_ref[...])
    dc = zc.shape[1] // 2
    u_ref[...] = zc[:, :dc] * jax.nn.sigmoid(zc[:, dc:])
    zr_ref[...] = _dot(x, wrkv_ref[...])
    zl_ref[...] = _dot(x, wlo_ref[...])


def _in_proj(xp, xs, pt, wcv, wrkv, wlo):
    nb, t, d = xp.shape
    n_tiles = t // GROUP
    n = n_tiles * ROWS + xs.shape[0]
    dc, dr, dl = wcv.shape[1] // 2, wrkv.shape[1], wlo.shape[1]
    row = lambda w: pl.BlockSpec((ROWS, w), lambda i: (i, 0))
    return pl.pallas_call(
        _in_proj_kernel,
        grid=(n_tiles + 1,),
        in_specs=[pl.BlockSpec((nb, GROUP, d), lambda i: (0, jnp.minimum(i, n_tiles - 1), 0)),
                  _full(xs.shape), _full(pt.shape), _full(wcv.shape), _full(wrkv.shape), _full(wlo.shape)],
        out_specs=[row(dc), row(dr), row(dl)],
        out_shape=[jax.ShapeDtypeStruct((n, dc), F32), jax.ShapeDtypeStruct((n, dr), F32),
                   jax.ShapeDtypeStruct((n, dl), F32)],
        compiler_params=_params("parallel"),
        name="in_proj",
    )(xp, xs, pt, wcv, wrkv, wlo)


def _conv_tail(acc, g, b):
    return _silu(_layer_norm(acc, g, b)).astype(BF16)


def _conv_seq_kernel(u_ref, w_ref, b_ref, g_ref, beta_ref, c_ref, hist_ref, *, taps, chunk):
    rows = u_ref.shape[0]
    hr = (taps - 1) * N_SEQ

    @pl.when(pl.program_id(0) == 0)
    def _():
        hist_ref[0:hr, :] = jnp.zeros((hr, hist_ref.shape[1]), F32)

    hist_ref[hr:hr + rows, :] = u_ref[...]

    def body(ci, carry):
        r0 = pl.multiple_of(ci * chunk, chunk)
        acc = jnp.broadcast_to(b_ref[...], (chunk, b_ref.shape[1]))
        for j in range(taps):
            acc = acc + w_ref[j:j + 1, :] * hist_ref[pl.ds(pl.multiple_of(r0 + j * N_SEQ, N_SEQ), chunk), :]
        c_ref[pl.ds(r0, chunk), :] = _conv_tail(acc, g_ref[...], beta_ref[...])
        return carry

    lax.fori_loop(0, rows // chunk, body, 0)
    hist_ref[0:hr, :] = hist_ref[rows:rows + hr, :]


def _conv_seq(u, n_rows, w, b, g, beta):
    taps, dc = w.shape
    rows = SEQ_TILE_ROWS
    hr = (taps - 1) * N_SEQ
    assert rows >= hr and n_rows % rows == 0
    return pl.pallas_call(
        functools.partial(_conv_seq_kernel, taps=taps, chunk=64),
        grid=(n_rows // rows,),
        in_specs=[pl.BlockSpec((rows, dc), lambda i: (i, 0)), _full(w.shape), _full(b.shape),
                  _full(g.shape), _full(beta.shape)],
        out_specs=pl.BlockSpec((rows, dc), lambda i: (i, 0)),
        out_shape=jax.ShapeDtypeStruct((n_rows, dc), BF16),
        scratch_shapes=[pltpu.VMEM((hr + rows, dc), F32)],
        compiler_params=_params("arbitrary"),
        name="conv_seq",
    )(u, w, b, g, beta)


def _conv_batch_kernel(cs_ref, u_ref, w_ref, b_ref, g_ref, beta_ref, c_ref, *, taps):
    acc = b_ref[...] + w_ref[taps - 1:taps, :] * u_ref[...]
    for j in range(taps - 1):
        acc = acc + w_ref[j:j + 1, :] * cs_ref[j]
    c_ref[...] = _conv_tail(acc, g_ref[...], beta_ref[...])


def _conv_batch(conv_t, u, row0, w, b, g, beta):
    taps, dc = w.shape
    ns = conv_t.shape[1]
    bs = 32
    assert ns % bs == 0 and row0 % bs == 0
    return pl.pallas_call(
        functools.partial(_conv_batch_kernel, taps=taps),
        grid=(ns // bs,),
        in_specs=[pl.BlockSpec((taps - 1, bs, dc), lambda i: (0, i, 0)),
                  pl.BlockSpec((bs, dc), lambda i: (row0 // bs + i, 0)),
                  _full(w.shape), _full(b.shape), _full(g.shape), _full(beta.shape)],
        out_specs=pl.BlockSpec((bs, dc), lambda i: (i, 0)),
        out_shape=jax.ShapeDtypeStruct((ns, dc), BF16),
        compiler_params=_params("parallel"),
        name="conv_batch",
    )(conv_t, u, w, b, g, beta)


def _rwkv_pre_math(zr, zl, zr_prev, zl_prev, mur, mul, w0, a0, kk_w, ka_w, rk_w, wup, aup, gup, ones_bd):
    dr = w0.shape[1]
    zsr = zr + (zr_prev - zr) * mur
    zsl = zl + (zl_prev - zl) * mul
    r, k, v = zsr[:, :dr], zsr[:, dr:2 * dr], zsr[:, 2 * dr:]
    lw = w0 + _dot(jnp.tanh(zsl).astype(BF16), wup)
    w_log = -(jnp.maximum(-lw, 0.0) + jnp.log(1.0 + jnp.exp(-jnp.abs(lw)))) - 0.5
    decay = jnp.exp(-jnp.exp(w_log))
    a = jax.nn.sigmoid(a0 + _dot(zsl.astype(BF16), aup))
    g = _dot(jax.nn.sigmoid(zsl).astype(BF16), gup)
    kk = k * kk_w
    ss = _split_dot(kk * kk, ones_bd)
    kk = kk / jnp.maximum(jnp.sqrt(ss), 1e-12)
    k2 = k * (1.0 + (a - 1.0) * ka_w)
    bonus = _split_dot(r * k2 * rk_w, ones_bd) * v
    return r, decay, k2, -kk, kk * a, v, g, bonus


N_PRE_PARAMS = 11
N_PRE_OUTS = 8


def _rwkv_pre_seq_kernel(zr_ref, zl_ref, *refs):
    prm, outs = refs[:N_PRE_PARAMS], refs[N_PRE_PARAMS:N_PRE_PARAMS + N_PRE_OUTS]
    car_r, car_l = refs[N_PRE_PARAMS + N_PRE_OUTS:]

    @pl.when(pl.program_id(0) == 0)
    def _():
        car_r[...] = jnp.zeros(car_r.shape, F32)
        car_l[...] = jnp.zeros(car_l.shape, F32)

    zr, zl = zr_ref[...], zl_ref[...]
    zr_prev = jnp.concatenate([car_r[...], zr[:-N_SEQ]], axis=0)
    zl_prev = jnp.concatenate([car_l[...], zl[:-N_SEQ]], axis=0)
    res = _rwkv_pre_math(zr, zl, zr_prev, zl_prev, *[p[...] for p in prm])
    for o, val in zip(outs, res):
        o[...] = val
    car_r[...] = zr[-N_SEQ:]
    car_l[...] = zl[-N_SEQ:]


def _rwkv_pre_batch_kernel(zr_ref, zl_ref, sr_ref, sl_ref, *refs):
    prm, outs = refs[:N_PRE_PARAMS], refs[N_PRE_PARAMS:]
    res = _rwkv_pre_math(zr_ref[...], zl_ref[...], sr_ref[...], sl_ref[...], *[p[...] for p in prm])
    for o, val in zip(outs, res):
        o[...] = val


def _rwkv_pre(zr, zl, n_rows, row0, rows, prm, state=None):
    dr3, dl = zr.shape[1], zl.shape[1]
    dr = dr3 // 3
    assert n_rows % rows == 0 and row0 % rows == 0
    zspec = lambda w: pl.BlockSpec((rows, w), lambda i: (row0 // rows + i, 0))
    ospec = pl.BlockSpec((rows, dr), lambda i: (i, 0))
    common = dict(
        grid=(n_rows // rows,),
        out_specs=[ospec] * N_PRE_OUTS,
        out_shape=[jax.ShapeDtypeStruct((n_rows, dr), F32)] * N_PRE_OUTS,
    )
    assert len(prm) == N_PRE_PARAMS
    pspecs = [_full(p.shape) for p in prm]
    if state is None:
        return pl.pallas_call(
            _rwkv_pre_seq_kernel,
            in_specs=[zspec(dr3), zspec(dl)] + pspecs,
            scratch_shapes=[pltpu.VMEM((N_SEQ, dr3), F32), pltpu.VMEM((N_SEQ, dl), F32)],
            compiler_params=_params("arbitrary"),
            name="rwkv_pre_seq",
            **common,
        )(zr, zl, *prm)
    sr, sl = state
    sspec = lambda w: pl.BlockSpec((rows, w), lambda i: (i, 0))
    return pl.pallas_call(
        _rwkv_pre_batch_kernel,
        in_specs=[zspec(dr3), zspec(dl), sspec(dr3), sspec(dl)] + pspecs,
        compiler_params=_params("parallel"),
        name="rwkv_pre_batch",
        **common,
    )(zr, zl, sr, sl, *prm)


def _wkv_kernel(r_ref, w_ref, k_ref, a_ref, b_ref, ve_ref, vo_ref, win_ref, hm_ref, s0_ref,
                y_ref, st_ref, s_scr, *, n_tok):
    n_groups = ve_ref.shape[0]
    hm = hm_ref[...]
    zeros8 = jnp.zeros(hm.shape, F32)

    @pl.when(pl.program_id(1) == 0)
    def _():
        s_scr[...] = s0_ref[...]

    def masked_rows(x8, b):
        return x8[b:b + 1, :] * hm

    def group(gi, carry):
        yacc = jnp.zeros((N_SEQ * HEAD_DIM, LANES), F32)
        for j in range(n_tok + 1):
            t = gi * n_tok + j
            blocks = [zeros8] * 32
            if j < n_tok:
                a8 = a_ref[t]
                for b in range(N_SEQ):
                    blocks[((16 * b + 8 * j + 8) % LANES) // 8] = masked_rows(a8, b)
            if j > 0:
                r8 = r_ref[t - 1]
                for b in range(N_SEQ):
                    blocks[16 + ((16 * b + 8 * (j - 1)) % LANES) // 8] = masked_rows(r8, b)
            rt = jnp.concatenate(blocks, axis=0).astype(BF16)
            c = _dot_nt(s_scr[...].astype(BF16), rt)
            if j > 0:
                yacc = yacc + c[:, LANES:] * win_ref[j - 1]
            if j < n_tok:
                vcol = ve_ref[gi] if j % 2 == 0 else vo_ref[gi]
                lhs2 = ((c[:, :LANES] + vcol) * win_ref[j]).astype(BF16)
                b8, k8, w8 = b_ref[t], k_ref[t], w_ref[t]
                blocks2 = [zeros8] * 16
                for b in range(N_SEQ):
                    blocks2[((16 * b + 8 * j) % LANES) // 8] = masked_rows(k8, b)
                    blocks2[((16 * b + 8 * j + 8) % LANES) // 8] = masked_rows(b8, b)
                rhs2 = jnp.concatenate(blocks2, axis=0).astype(BF16)
                d = _dot(lhs2, rhs2)
                for b in range(N_SEQ):
                    sl = slice(b * HEAD_DIM, (b + 1) * HEAD_DIM)
                    s_scr[sl, :] = s_scr[sl, :] * w8[b:b + 1, :] + d[sl, :]
        y_ref[gi] = yacc
        return carry

    lax.fori_loop(0, n_groups, group, 0)

    @pl.when(pl.program_id(1) == pl.num_programs(1) - 1)
    def _():
        st_ref[...] = s_scr[...]


def _wkv(r, w, k, a, b, ve, vo, win, hm, s0, tokens):
    t, nseq, dr = r.shape
    n_tok = min(GROUP, tokens)
    assert t % tokens == 0 and tokens % n_tok == 0 and nseq % N_SEQ == 0
    assert dr // HEAD_DIM == LANES // GROUP
    gps = tokens // n_tok
    srows = N_SEQ * HEAD_DIM
    tspec = pl.BlockSpec((tokens, N_SEQ, dr), lambda bb, c: (c, bb, 0))
    vspec = pl.BlockSpec((gps, srows, LANES), lambda bb, c: (c, bb, 0))
    sspec = pl.BlockSpec((srows, dr), lambda bb, c: (bb, 0))
    return pl.pallas_call(
        functools.partial(_wkv_kernel, n_tok=n_tok),
        grid=(nseq // N_SEQ, t // tokens),
        in_specs=[tspec] * 5 + [vspec, vspec, _full(win.shape), _full(hm.shape), sspec],
        out_specs=[vspec, sspec],
        out_shape=[jax.ShapeDtypeStruct(ve.shape, F32), jax.ShapeDtypeStruct(s0.shape, F32)],
        scratch_shapes=[pltpu.VMEM((srows, dr), F32)],
        compiler_params=_params("parallel", "arbitrary"),
        name="wkv",
    )(r, w, k, a, b, ve, vo, win, hm, s0)


def _window_masks():
    lane = np.arange(LANES)[None, None, :]
    b = (np.arange(N_SEQ * HEAD_DIM) // HEAD_DIM)[None, :, None]
    j = np.arange(GROUP)[:, None, None]
    return jnp.asarray(((lane - 16 * b - 8 * j) % LANES) < 16, F32)


def _to_columns(v, n_tok):
    t, nseq, dr = v.shape
    h = dr // HEAD_DIM
    g = t // n_tok
    x = v.reshape(g, n_tok, nseq // N_SEQ, N_SEQ, h, HEAD_DIM)
    x = jnp.pad(x, ((0, 0), (0, GROUP - n_tok)) + ((0, 0),) * 4)
    x = x.transpose(0, 2, 3, 5, 1, 4).reshape(g, nseq // N_SEQ, N_SEQ, HEAD_DIM, LANES)
    x = jnp.stack([jnp.roll(x[:, :, b], 16 * b, axis=-1) for b in range(N_SEQ)], axis=2)
    x = x.reshape(g, nseq * HEAD_DIM, LANES)
    even = ((jnp.arange(LANES) // 8) % 2 == 0)
    return jnp.where(even, x, 0.0), jnp.where(even, 0.0, x)


def _from_columns(y, n_tok, nseq):
    g = y.shape[0]
    x = y.reshape(g, nseq // N_SEQ, N_SEQ, HEAD_DIM, LANES)
    x = jnp.stack([jnp.roll(x[:, :, b], -16 * b, axis=-1) for b in range(N_SEQ)], axis=2)
    x = x.reshape(g, nseq // N_SEQ, N_SEQ, HEAD_DIM, GROUP, LANES // GROUP)[:, :, :, :, :n_tok]
    x = x.transpose(0, 4, 1, 2, 5, 3)
    return x.reshape(g * n_tok, nseq, (LANES // GROUP) * HEAD_DIM)


def _post_kernel(xp_ref, xs_ref, cp_ref, cs_ref, yp_ref, ys_ref, gp_ref, gs_ref, bp_ref, bs_ref,
                 p_ref, avg_ref, gng_ref, gnb_ref, wout_ref, ln_g_ref, ln_b_ref, x1_ref, x1b_ref, *, alpha):
    i = pl.program_id(0)
    is_s = i == pl.num_programs(0) - 1
    pick = lambda p, s: jnp.where(is_s, s[...], p[...])
    c, y, g, bonus = pick(cp_ref, cs_ref), pick(yp_ref, ys_ref), pick(gp_ref, gs_ref), pick(bp_ref, bs_ref)
    mu = _split_dot(y, avg_ref[...])
    d = y - mu
    var = _split_dot(d * d, avg_ref[...])
    yn = d * lax.rsqrt(var + GN_EPS) * gng_ref[...] + gnb_ref[...]
    o = ((yn + bonus) * g).astype(BF16)
    lhs = jnp.concatenate([c, o], axis=1)
    lhs = jnp.where(is_s, lhs, _dot(p_ref[...], lhs).astype(BF16))
    h = _dot(lhs, wout_ref[...])
    x = jnp.where(is_s, xs_ref[...], xp_ref[...].reshape(ROWS, xp_ref.shape[-1]))
    x1 = _layer_norm(alpha * x + h, ln_g_ref[...], ln_b_ref[...])
    x1_ref[...] = x1
    x1b_ref[...] = x1.astype(BF16)


def _post(xp, xs, cp, cs, yp, ys, gp, gs, bp, bs, perm, avg, gng, gnb, wout, ln_g, ln_b, alpha):
    nb, t, d = xp.shape
    n_tiles = t // GROUP
    n = n_tiles * ROWS + xs.shape[0]
    dr = yp.shape[1]
    pm = lambda w: pl.BlockSpec((ROWS, w), lambda i: (jnp.minimum(i, n_tiles - 1), 0))
    sm = lambda w: pl.BlockSpec((ROWS, w), lambda i: (0, 0))
    consts = [perm, avg, gng, gnb, wout, ln_g, ln_b]
    return pl.pallas_call(
        functools.partial(_post_kernel, alpha=alpha),
        grid=(n_tiles + 1,),
        in_specs=[pl.BlockSpec((nb, GROUP, d), lambda i: (0, jnp.minimum(i, n_tiles - 1), 0)), sm(d),
                  pm(cp.shape[1]), sm(cp.shape[1]), pm(dr), sm(dr), pm(dr), sm(dr), pm(dr), sm(dr)]
                 + [_full(a.shape) for a in consts],
        out_specs=[pl.BlockSpec((ROWS, d), lambda i: (i, 0))] * 2,
        out_shape=[jax.ShapeDtypeStruct((n, d), F32), jax.ShapeDtypeStruct((n, d), BF16)],
        compiler_params=_params("parallel"),
        name="post",
    )(xp, xs, cp, cs, yp, ys, gp, gs, bp, bs, *consts)


def _reduce01(fn, x):
    return fn(fn(x, axis=1, keepdims=True), axis=0, keepdims=True)


def _router_kernel(x_ref, wt_ref, bias_ref, gt_ref):
    n_exp = wt_ref.shape[0]
    per = n_exp // N_GROUPS
    logits = _dot_nt(wt_ref[...], x_ref[...])
    scores = jax.nn.sigmoid(logits)
    sel = (scores + bias_ref[...]).reshape(N_GROUPS, per, ROWS)
    sc3 = scores.reshape(N_GROUPS, per, ROWS)
    neg = -jnp.inf
    pidx = lax.broadcasted_iota(jnp.int32, sel.shape, 1)
    m1 = jnp.max(sel, axis=1, keepdims=True)
    i1 = jnp.min(jnp.where(sel == m1, pidx, per), axis=1, keepdims=True)
    m2 = jnp.max(jnp.where(pidx == i1, neg, sel), axis=1, keepdims=True)
    gscore = m1 + m2
    gidx = lax.broadcasted_iota(jnp.int32, gscore.shape, 0)
    gkeep = jnp.zeros(gscore.shape, F32)
    for _ in range(TOPK_GROUPS):
        gm = jnp.max(gscore, axis=0, keepdims=True)
        gi = jnp.min(jnp.where(gscore == gm, gidx, N_GROUPS), axis=0, keepdims=True)
        hit = gidx == gi
        gkeep = jnp.where(hit, 1.0, gkeep)
        gscore = jnp.where(hit, neg, gscore)
    cand = jnp.where(gkeep > 0.0, sel, neg)
    eidx = lax.broadcasted_iota(jnp.int32, sel.shape, 0) * per + pidx
    gates = jnp.zeros(sel.shape, F32)
    for _ in range(TOP_K):
        em = _reduce01(jnp.max, cand)
        ei = _reduce01(jnp.min, jnp.where(cand == em, eidx, n_exp))
        hit = eidx == ei
        gates = jnp.where(hit, sc3, gates)
        cand = jnp.where(hit, neg, cand)
    total = _reduce01(jnp.sum, gates)
    gates = gates / (total + 1e-20) * ROUTED_SCALE
    gt_ref[...] = gates.reshape(n_exp, ROWS)


def _router(x1b, wt, bias_b):
    n, d = x1b.shape
    n_exp = wt.shape[0]
    return pl.pallas_call(
        _router_kernel,
        grid=(n // ROWS,),
        in_specs=[pl.BlockSpec((ROWS, d), lambda i: (i, 0)), _full(wt.shape), _full(bias_b.shape)],
        out_specs=pl.BlockSpec((n_exp, ROWS), lambda i: (0, i)),
        out_shape=jax.ShapeDtypeStruct((n_exp, n), F32),
        compiler_params=_params("parallel"),
        name="router",
    )(x1b, wt, bias_b)


def _experts_kernel(x_ref, g_ref, wg_ref, wu_ref, wd_ref, sg_ref, su_ref, sd_ref, f_ref, acc_ref):
    e = pl.program_id(1)
    x = x_ref[...]

    @pl.when(e == 0)
    def _():
        hs = _silu(_dot(x, sg_ref[...])) * _dot(x, su_ref[...])
        acc_ref[...] = _dot(hs.astype(BF16), sd_ref[...])

    h = _silu(_dot(x, wg_ref[0].astype(BF16))) * _dot(x, wu_ref[0].astype(BF16))
    lane = lax.broadcasted_iota(jnp.int32, g_ref.shape, 1)
    gcol = jnp.sum(jnp.where(lane == e, g_ref[...], 0.0), axis=1, keepdims=True)
    acc_ref[...] += _dot((h * gcol).astype(BF16), wd_ref[0].astype(BF16))

    @pl.when(e == pl.num_programs(1) - 1)
    def _():
        f_ref[...] = acc_ref[...]


def _experts(x1b, gates, wg, wu, wd, sg, su, sd):
    n, d = x1b.shape
    n_exp = wg.shape[0]
    rows = max(r for r in range(16, min(n, MOE_ROWS) + 1, 16) if n % r == 0)
    wspec = lambda a: pl.BlockSpec((1,) + a.shape[1:], lambda i, e: (e, 0, 0))
    return pl.pallas_call(
        _experts_kernel,
        grid=(n // rows, n_exp),
        in_specs=[pl.BlockSpec((rows, d), lambda i, e: (i, 0)), pl.BlockSpec((rows, n_exp), lambda i, e: (i, 0)),
                  wspec(wg), wspec(wu), wspec(wd), _full(sg.shape), _full(su.shape), _full(sd.shape)],
        out_specs=pl.BlockSpec((rows, d), lambda i, e: (i, 0)),
        out_shape=jax.ShapeDtypeStruct((n, d), F32),
        scratch_shapes=[pltpu.VMEM((rows, d), F32)],
        compiler_params=_params("parallel", "arbitrary"),
        name="experts",
    )(x1b, gates, wg, wu, wd, sg, su, sd)


def _final_kernel(x1_ref, f_ref, g_ref, b_ref, yp_ref, ys_ref, *, alpha):
    i = pl.program_id(0)
    last = pl.num_programs(0) - 1
    y = _layer_norm(alpha * x1_ref[...] + f_ref[...], g_ref[...], b_ref[...])

    @pl.when(i < last)
    def _():
        yp_ref[...] = y.reshape(yp_ref.shape)

    @pl.when(i == last)
    def _():
        ys_ref[...] = y


def _final(x1, f, g, b, nb, t, alpha):
    n, d = x1.shape
    n_tiles = t // GROUP
    ns = n - n_tiles * ROWS
    assert ns == ROWS
    row = pl.BlockSpec((ROWS, d), lambda i: (i, 0))
    return pl.pallas_call(
        functools.partial(_final_kernel, alpha=alpha),
        grid=(n_tiles + 1,),
        in_specs=[row, row, _full(g.shape), _full(b.shape)],
        out_specs=[pl.BlockSpec((nb, GROUP, d), lambda i: (0, jnp.minimum(i, n_tiles - 1), 0)),
                   pl.BlockSpec((ns, d), lambda i: (0, 0))],
        out_shape=[jax.ShapeDtypeStruct((nb, t, d), F32), jax.ShapeDtypeStruct((ns, d), F32)],
        compiler_params=_params("arbitrary"),
        name="final",
    )(x1, f, g, b)


def _tile_permutation():
    p = np.zeros((ROWS, ROWS), np.float32)
    for b in range(N_SEQ):
        for t in range(GROUP):
            p[b * GROUP + t, t * N_SEQ + b] = 1.0
    return p


def _row(a):
    return a.reshape(1, -1).astype(F32)


def _layer(xp, xs, conv_s, shift_s, wkv_s, lp, alpha):
    nb, t, d = xp.shape
    ns = xs.shape[0]
    assert nb == N_SEQ and ns == ROWS and t % WKV_TOKENS == 0
    n_p = nb * t
    dc = lp["conv_w"].shape[1]
    dr = lp["w0"].shape[0]
    n_heads = dr // HEAD_DIM
    lw, la, lg = lp["w_up"].shape[0], lp["a_up"].shape[0], lp["g_up"].shape[0]
    lora = lw + la + lg
    lora_pad = -(-lora // LANES) * LANES

    w_in = lp["w_in"]
    wcv = w_in[:, :2 * dc].astype(BF16)
    wrkv = w_in[:, 2 * dc:2 * dc + 3 * dr].astype(BF16)
    wlo = jnp.pad(w_in[:, 2 * dc + 3 * dr:], ((0, 0), (0, lora_pad - lora))).astype(BF16)
    mu = lp["mu_shift"]
    mur = _row(mu[:3 * dr])
    mul = _row(jnp.pad(mu[3 * dr:], (0, lora_pad - lora)))
    wup = jnp.pad(lp["w_up"], ((0, lora_pad - lw), (0, 0))).astype(BF16)
    aup = jnp.pad(lp["a_up"], ((lw, lora_pad - lw - la), (0, 0))).astype(BF16)
    gup = jnp.pad(lp["g_up"], ((lw + la, lora_pad - lora), (0, 0))).astype(BF16)
    head_of = np.arange(dr) // HEAD_DIM
    same_head = (head_of[:, None] == head_of[None, :]).astype(np.float32)
    ones_bd = jnp.asarray(same_head, BF16)
    avg_bd = jnp.asarray(same_head / HEAD_DIM, BF16)
    perm = _tile_permutation()
    prm = [mur, mul, _row(lp["w0"]), _row(lp["a0"]), _row(lp["k_k"]), _row(lp["k_a"]), _row(lp["r_k"]),
           wup, aup, gup, ones_bd]
    conv_prm = [lp["conv_w"].astype(F32), _row(lp["conv_b"]), _row(lp["conv_ln_g"]), _row(lp["conv_ln_b"])]

    u, zr, zl = _in_proj(xp, xs.reshape(ns, d), jnp.asarray(perm.T, BF16), wcv, wrkv, wlo)

    c_p = _conv_seq(u, n_p, *conv_prm)
    c_s = _conv_batch(conv_s.transpose(1, 0, 2), u, n_p, *conv_prm)

    shift_r = shift_s[:, :3 * dr]
    shift_l = jnp.pad(shift_s[:, 3 * dr:], ((0, 0), (0, lora_pad - lora)))
    pre_p = _rwkv_pre(zr, zl, n_p, 0, SEQ_TILE_ROWS, prm)
    pre_s = _rwkv_pre(zr, zl, ns, n_p, ROWS, prm, state=(shift_r, shift_l))

    win = _window_masks()
    hm = jnp.asarray(head_of[None, :] == np.arange(n_heads)[:, None], F32)

    def run_wkv(pre, nseq, tt, tokens, s0):
        r, w, k, a, b, v = [z.reshape(tt, nseq, dr) for z in pre[:6]]
        n_tok = min(GROUP, tokens)
        ve, vo = _to_columns(v, n_tok)
        ycol, s_new = _wkv(r, w, k, a, b, ve, vo, win, hm, s0, tokens)
        return _from_columns(ycol, n_tok, nseq).reshape(tt * nseq, dr), s_new

    y_p, st_p = run_wkv(pre_p, nb, t, WKV_TOKENS, jnp.zeros((nb * HEAD_DIM, dr), F32))
    s0_s = wkv_s.transpose(0, 2, 1, 3).reshape(ns * HEAD_DIM, dr)
    y_s, st_s = run_wkv(pre_s, ns, 1, 1, s0_s)

    x1, x1b = _post(xp, xs.reshape(ns, d), c_p, c_s, y_p, y_s, pre_p[6], pre_s[6], pre_p[7], pre_s[7],
                    jnp.asarray(perm, BF16), avg_bd, _row(lp["gn_g"]), _row(lp["gn_b"]),
                    lp["w_out"].astype(BF16), _row(lp["ln1_g"]), _row(lp["ln1_b"]), alpha)

    n_exp = lp["router_w"].shape[1]
    bias_b = jnp.broadcast_to(lp["router_bias"].astype(F32)[:, None], (n_exp, ROWS))
    gates_t = _router(x1b, lp["router_w"].T.astype(BF16), bias_b)
    f = _experts(x1b, gates_t.T, lp["exp_gate"], lp["exp_up"], lp["exp_down"],
                 lp["sh_gate"].astype(BF16), lp["sh_up"].astype(BF16), lp["sh_down"].astype(BF16))
    y_prompt, y_sample = _final(x1, f, _row(lp["ln2_g"]), _row(lp["ln2_b"]), nb, t, alpha)

    taps = lp["conv_w"].shape[0]
    new_conv_p = u[n_p - (taps - 1) * nb:n_p].reshape(taps - 1, nb, dc).transpose(1, 0, 2)
    new_conv_s = jnp.concatenate([conv_s[:, 1:], u[n_p:, None, :]], axis=1)
    z_last = jnp.concatenate([zr, zl[:, :lora]], axis=1)
    new_shift_p = z_last[n_p - nb:n_p]
    new_shift_s = z_last[n_p:]
    unstate = lambda s, nseq: s.reshape(nseq, HEAD_DIM, n_heads, HEAD_DIM).transpose(0, 2, 1, 3)
    return (y_prompt, y_sample.reshape(ns, 1, d), new_conv_p, new_shift_p, unstate(st_p, nb),
            new_conv_s, new_shift_s, unstate(st_s, ns))


def kernel(x_prompt, x_sample, state_conv, state_shift, state_wkv, w_in, mu_shift, conv_w, conv_b, conv_ln_g, conv_ln_b, w0, w_up, a0, a_up, g_up, k_k, k_a, r_k, gn_g, gn_b, w_out, ln1_g, ln1_b, router_w, router_bias, exp_gate, exp_up, exp_down, sh_gate, sh_up, sh_down, ln2_g, ln2_b):
    depth = w_in.shape[0]
    alpha = (2.0 * depth) ** 0.25
    names = ("w_in", "mu_shift", "conv_w", "conv_b", "conv_ln_g", "conv_ln_b", "w0", "w_up", "a0", "a_up", "g_up",
             "k_k", "k_a", "r_k", "gn_g", "gn_b", "w_out", "ln1_g", "ln1_b", "router_w", "router_bias", "exp_gate",
             "exp_up", "exp_down", "sh_gate", "sh_up", "sh_down", "ln2_g", "ln2_b")
    vals = (w_in, mu_shift, conv_w, conv_b, conv_ln_g, conv_ln_b, w0, w_up, a0, a_up, g_up, k_k, k_a, r_k, gn_g,
            gn_b, w_out, ln1_g, ln1_b, router_w, router_bias, exp_gate, exp_up, exp_down, sh_gate, sh_up, sh_down,
            ln2_g, ln2_b)
    xp, xs = x_prompt, x_sample
    outs = [[] for _ in range(6)]
    for l in range(depth):
        lp = {n: v[l] for n, v in zip(names, vals)}
        assert xs.shape[1] == 1
        xp, xs, *states = _layer(xp, xs, state_conv[l], state_shift[l], state_wkv[l], lp, alpha)
        for o, s in zip(outs, states):
            o.append(s)
    return (xp, xs) + tuple(jnp.stack(o) for o in outs)
```

```python
import functools

import jax
import jax.numpy as jnp
import numpy as np
from jax import lax
from jax.experimental import pallas as pl
from jax.experimental.pallas import tpu as pltpu

F32 = jnp.float32
BF16 = jnp.bfloat16

HEAD_DIM = 64
N_SEQ = 8
GROUP = 16
ROWS = GROUP * N_SEQ
LANES = 128
LN_EPS = 1e-5
GN_EPS = 64e-5
ROUTED_SCALE = 2.5
N_GROUPS = 8
TOPK_GROUPS = 4
TOP_K = 8
SEQ_TILE_ROWS = 512
WKV_TOKENS = 64
MOE_ROWS = 1376
VMEM_LIMIT = 48 * 1024 * 1024
MOE_VMEM_LIMIT = 56 * 1024 * 1024


def _params(*sem):
    return pltpu.CompilerParams(dimension_semantics=sem, vmem_limit_bytes=VMEM_LIMIT)


def _full(shape):
    n = len(shape)
    return pl.BlockSpec(shape, lambda *_: (0,) * n)


def _dot(a, b):
    return jnp.dot(a, b, preferred_element_type=F32)


def _dot_nt(a, b):
    return lax.dot_general(a, b, (((1,), (1,)), ((), ())), preferred_element_type=F32)


def _split_dot(x, m):
    hi = x.astype(BF16)
    lo = (x - hi.astype(F32)).astype(BF16)
    return _dot(hi, m) + _dot(lo, m)


def _layer_norm(x, g, b):
    mu = jnp.mean(x, axis=-1, keepdims=True)
    d = x - mu
    var = jnp.mean(d * d, axis=-1, keepdims=True)
    return d * lax.rsqrt(var + LN_EPS) * g + b


def _silu(x):
    return x * jax.nn.sigmoid(x)


def _in_proj_kernel(xp_ref, xs_ref, pt_ref, wcv_ref, wrkv_ref, wlo_ref, u_ref, zr_ref, zl_ref):
    i = pl.program_id(0)
    last = pl.num_programs(0) - 1
    xb = xp_ref[...].reshape(ROWS, xp_ref.shape[-1]).astype(BF16)
    xt = _dot(pt_ref[...], xb).astype(BF16)
    x = jnp.where(i == last, xs_ref[...].astype(BF16), xt)
    zc = _dot(x, wcv_ref[...])
    dc = zc.shape[1] // 2
    u_ref[...] = zc[:, :dc] * jax.nn.sigmoid(zc[:, dc:])
    zr_ref[...] = _dot(x, wrkv_ref[...])
    zl_ref[...] = _dot(x, wlo_ref[...])


def _in_proj(xp, xs, pt, wcv, wrkv, wlo):
    nb, t, d = xp.shape
    n_tiles = t // GROUP
    n = n_tiles * ROWS + xs.shape[0]
    dc, dr, dl = wcv.shape[1] // 2, wrkv.shape[1], wlo.shape[1]
    row = lambda w: pl.BlockSpec((ROWS, w), lambda i: (i, 0))
    return pl.pallas_call(
        _in_proj_kernel,
        grid=(n_tiles + 1,),
        in_specs=[pl.BlockSpec((nb, GROUP, d), lambda i: (0, jnp.minimum(i, n_tiles - 1), 0)),
                  _full(xs.shape), _full(pt.shape), _full(wcv.shape), _full(wrkv.shape), _full(wlo.shape)],
        out_specs=[row(dc), row(dr), row(dl)],
        out_shape=[jax.ShapeDtypeStruct((n, dc), F32), jax.ShapeDtypeStruct((n, dr), F32),
                   jax.ShapeDtypeStruct((n, dl), F32)],
        compiler_params=_params("parallel"),
        name="in_proj",
    )(xp, xs, pt, wcv, wrkv, wlo)


def _conv_tail(acc, g, b):
    return _silu(_layer_norm(acc, g, b)).astype(BF16)


def _conv_seq_kernel(u_ref, w_ref, b_ref, g_ref, beta_ref, c_ref, hist_ref, *, taps, chunk):
    rows = u_ref.shape[0]
    hr = (taps - 1) * N_SEQ

    @pl.when(pl.program_id(0) == 0)
    def _():
        hist_ref[0:hr, :] = jnp.zeros((hr, hist_ref.shape[1]), F32)

    hist_ref[hr:hr + rows, :] = u_ref[...]

    def body(ci, carry):
        r0 = pl.multiple_of(ci * chunk, chunk)
        acc = jnp.broadcast_to(b_ref[...], (chunk, b_ref.shape[1]))
        for j in range(taps):
            acc = acc + w_ref[j:j + 1, :] * hist_ref[pl.ds(pl.multiple_of(r0 + j * N_SEQ, N_SEQ), chunk), :]
        c_ref[pl.ds(r0, chunk), :] = _conv_tail(acc, g_ref[...], beta_ref[...])
        return carry

    lax.fori_loop(0, rows // chunk, body, 0)
    hist_ref[0:hr, :] = hist_ref[rows:rows + hr, :]


def _conv_seq(u, n_rows, w, b, g, beta):
    taps, dc = w.shape
    rows = SEQ_TILE_ROWS
    hr = (taps - 1) * N_SEQ
    assert rows >= hr and n_rows % rows == 0
    return pl.pallas_call(
        functools.partial(_conv_seq_kernel, taps=taps, chunk=64),
        grid=(n_rows // rows,),
        in_specs=[pl.BlockSpec((rows, dc), lambda i: (i, 0)), _full(w.shape), _full(b.shape),
                  _full(g.shape), _full(beta.shape)],
        out_specs=pl.BlockSpec((rows, dc), lambda i: (i, 0)),
        out_shape=jax.ShapeDtypeStruct((n_rows, dc), BF16),
        scratch_shapes=[pltpu.VMEM((hr + rows, dc), F32)],
        compiler_params=_params("arbitrary"),
        name="conv_seq",
    )(u, w, b, g, beta)


def _conv_batch_kernel(cs_ref, u_ref, w_ref, b_ref, g_ref, beta_ref, c_ref, *, taps):
    acc = b_ref[...] + w_ref[taps - 1:taps, :] * u_ref[...]
    for j in range(taps - 1):
        acc = acc + w_ref[j:j + 1, :] * cs_ref[j]
    c_ref[...] = _conv_tail(acc, g_ref[...], beta_ref[...])


def _conv_batch(conv_t, u, row0, w, b, g, beta):
    taps, dc = w.shape
    ns = conv_t.shape[1]
    bs = 32
    assert ns % bs == 0 and row0 % bs == 0
    return pl.pallas_call(
        functools.partial(_conv_batch_kernel, taps=taps),
        grid=(ns // bs,),
        in_specs=[pl.BlockSpec((taps - 1, bs, dc), lambda i: (0, i, 0)),
                  pl.BlockSpec((bs, dc), lambda i: (row0 // bs + i, 0)),
                  _full(w.shape), _full(b.shape), _full(g.shape), _full(beta.shape)],
        out_specs=pl.BlockSpec((bs, dc), lambda i: (i, 0)),
        out_shape=jax.ShapeDtypeStruct((ns, dc), BF16),
        compiler_params=_params("parallel"),
        name="conv_batch",
    )(conv_t, u, w, b, g, beta)


def _rwkv_pre_math(zr, zl, zr_prev, zl_prev, mur, mul, w0, a0, kk_w, ka_w, rk_w, wup, aup, gup, ones_bd):
    dr = w0.shape[1]
    zsr = zr + (zr_prev - zr) * mur
    zsl = zl + (zl_prev - zl) * mul
    r, k, v = zsr[:, :dr], zsr[:, dr:2 * dr], zsr[:, 2 * dr:]
    lw = w0 + _dot(jnp.tanh(zsl).astype(BF16), wup)
    w_log = -(jnp.maximum(-lw, 0.0) + jnp.log(1.0 + jnp.exp(-jnp.abs(lw)))) - 0.5
    decay = jnp.exp(-jnp.exp(w_log))
    a = jax.nn.sigmoid(a0 + _dot(zsl.astype(BF16), aup))
    g = _dot(jax.nn.sigmoid(zsl).astype(BF16), gup)
    kk = k * kk_w
    ss = _split_dot(kk * kk, ones_bd)
    kk = kk / jnp.maximum(jnp.sqrt(ss), 1e-12)
    k2 = k * (1.0 + (a - 1.0) * ka_w)
    bonus = _split_dot(r * k2 * rk_w, ones_bd) * v
    return r, decay, k2, -kk, kk * a, v, g, bonus


N_PRE_PARAMS = 11
N_PRE_OUTS = 8


def _rwkv_pre_seq_kernel(zr_ref, zl_ref, *refs):
    prm, outs = refs[:N_PRE_PARAMS], refs[N_PRE_PARAMS:N_PRE_PARAMS + N_PRE_OUTS]
    car_r, car_l = refs[N_PRE_PARAMS + N_PRE_OUTS:]

    @pl.when(pl.program_id(0) == 0)
    def _():
        car_r[...] = jnp.zeros(car_r.shape, F32)
        car_l[...] = jnp.zeros(car_l.shape, F32)

    zr, zl = zr_ref[...], zl_ref[...]
    zr_prev = jnp.concatenate([car_r[...], zr[:-N_SEQ]], axis=0)
    zl_prev = jnp.concatenate([car_l[...], zl[:-N_SEQ]], axis=0)
    res = _rwkv_pre_math(zr, zl, zr_prev, zl_prev, *[p[...] for p in prm])
    for o, val in zip(outs, res):
        o[...] = val
    car_r[...] = zr[-N_SEQ:]
    car_l[...] = zl[-N_SEQ:]


def _rwkv_pre_batch_kernel(zr_ref, zl_ref, sr_ref, sl_ref, *refs):
    prm, outs = refs[:N_PRE_PARAMS], refs[N_PRE_PARAMS:]
    res = _rwkv_pre_math(zr_ref[...], zl_ref[...], sr_ref[...], sl_ref[...], *[p[...] for p in prm])
    for o, val in zip(outs, res):
        o[...] = val


def _rwkv_pre(zr, zl, n_rows, row0, rows, prm, state=None):
    dr3, dl = zr.shape[1], zl.shape[1]
    dr = dr3 // 3
    assert n_rows % rows == 0 and row0 % rows == 0
    zspec = lambda w: pl.BlockSpec((rows, w), lambda i: (row0 // rows + i, 0))
    ospec = pl.BlockSpec((rows, dr), lambda i: (i, 0))
    common = dict(
        grid=(n_rows // rows,),
        out_specs=[ospec] * N_PRE_OUTS,
        out_shape=[jax.ShapeDtypeStruct((n_rows, dr), F32)] * N_PRE_OUTS,
    )
    assert len(prm) == N_PRE_PARAMS
    pspecs = [_full(p.shape) for p in prm]
    if state is None:
        return pl.pallas_call(
            _rwkv_pre_seq_kernel,
            in_specs=[zspec(dr3), zspec(dl)] + pspecs,
            scratch_shapes=[pltpu.VMEM((N_SEQ, dr3), F32), pltpu.VMEM((N_SEQ, dl), F32)],
            compiler_params=_params("arbitrary"),
            name="rwkv_pre_seq",
            **common,
        )(zr, zl, *prm)
    sr, sl = state
    sspec = lambda w: pl.BlockSpec((rows, w), lambda i: (i, 0))
    return pl.pallas_call(
        _rwkv_pre_batch_kernel,
        in_specs=[zspec(dr3), zspec(dl), sspec(dr3), sspec(dl)] + pspecs,
        compiler_params=_params("parallel"),
        name="rwkv_pre_batch",
        **common,
    )(zr, zl, sr, sl, *prm)


def _wkv_kernel(r_ref, w_ref, k_ref, a_ref, b_ref, ve_ref, vo_ref, win_ref, hm_ref, s0_ref,
                y_ref, st_ref, s_scr, *, n_tok):
    n_groups = ve_ref.shape[0]
    hm = hm_ref[...]
    zeros8 = jnp.zeros(hm.shape, F32)

    @pl.when(pl.program_id(1) == 0)
    def _():
        s_scr[...] = s0_ref[...]

    def masked_rows(x8, b):
        return x8[b:b + 1, :] * hm

    def group(gi, carry):
        yacc = jnp.zeros((N_SEQ * HEAD_DIM, LANES), F32)
        for j in range(n_tok + 1):
            t = gi * n_tok + j
            blocks = [zeros8] * 32
            if j < n_tok:
                a8 = a_ref[t]
                for b in range(N_SEQ):
                    blocks[((16 * b + 8 * j + 8) % LANES) // 8] = masked_rows(a8, b)
            if j > 0:
                r8 = r_ref[t - 1]
                for b in range(N_SEQ):
                    blocks[16 + ((16 * b + 8 * (j - 1)) % LANES) // 8] = masked_rows(r8, b)
            rt = jnp.concatenate(blocks, axis=0).astype(BF16)
            c = _dot_nt(s_scr[...].astype(BF16), rt)
            if j > 0:
                yacc = yacc + c[:, LANES:] * win_ref[j - 1]
            if j < n_tok:
                vcol = ve_ref[gi] if j % 2 == 0 else vo_ref[gi]
                lhs2 = ((c[:, :LANES] + vcol) * win_ref[j]).astype(BF16)
                b8, k8, w8 = b_ref[t], k_ref[t], w_ref[t]
                blocks2 = [zeros8] * 16
                for b in range(N_SEQ):
                    blocks2[((16 * b + 8 * j) % LANES) // 8] = masked_rows(k8, b)
                    blocks2[((16 * b + 8 * j + 8) % LANES) // 8] = masked_rows(b8, b)
                rhs2 = jnp.concatenate(blocks2, axis=0).astype(BF16)
                d = _dot(lhs2, rhs2)
                for b in range(N_SEQ):
                    sl = slice(b * HEAD_DIM, (b + 1) * HEAD_DIM)
                    s_scr[sl, :] = s_scr[sl, :] * w8[b:b + 1, :] + d[sl, :]
        y_ref[gi] = yacc
        return carry

    lax.fori_loop(0, n_groups, group, 0)

    @pl.when(pl.program_id(1) == pl.num_programs(1) - 1)
    def _():
        st_ref[...] = s_scr[...]


def _wkv(r, w, k, a, b, ve, vo, win, hm, s0, tokens):
    t, nseq, dr = r.shape
    n_tok = min(GROUP, tokens)
    assert t % tokens == 0 and tokens % n_tok == 0 and nseq % N_SEQ == 0
    assert dr // HEAD_DIM == LANES // GROUP
    gps = tokens // n_tok
    srows = N_SEQ * HEAD_DIM
    tspec = pl.BlockSpec((tokens, N_SEQ, dr), lambda bb, c: (c, bb, 0))
    vspec = pl.BlockSpec((gps, srows, LANES), lambda bb, c: (c, bb, 0))
    sspec = pl.BlockSpec((srows, dr), lambda bb, c: (bb, 0))
    return pl.pallas_call(
        functools.partial(_wkv_kernel, n_tok=n_tok),
        grid=(nseq // N_SEQ, t // tokens),
        in_specs=[tspec] * 5 + [vspec, vspec, _full(win.shape), _full(hm.shape), sspec],
        out_specs=[vspec, sspec],
        out_shape=[jax.ShapeDtypeStruct(ve.shape, F32), jax.ShapeDtypeStruct(s0.shape, F32)],
        scratch_shapes=[pltpu.VMEM((srows, dr), F32)],
        compiler_params=_params("parallel", "arbitrary"),
        name="wkv",
    )(r, w, k, a, b, ve, vo, win, hm, s0)


def _window_masks():
    lane = np.arange(LANES)[None, None, :]
    b = (np.arange(N_SEQ * HEAD_DIM) // HEAD_DIM)[None, :, None]
    j = np.arange(GROUP)[:, None, None]
    return jnp.asarray(((lane - 16 * b - 8 * j) % LANES) < 16, F32)


def _to_columns(v, n_tok):
    t, nseq, dr = v.shape
    h = dr // HEAD_DIM
    g = t // n_tok
    x = v.reshape(g, n_tok, nseq // N_SEQ, N_SEQ, h, HEAD_DIM)
    x = jnp.pad(x, ((0, 0), (0, GROUP - n_tok)) + ((0, 0),) * 4)
    x = x.transpose(0, 2, 3, 5, 1, 4).reshape(g, nseq // N_SEQ, N_SEQ, HEAD_DIM, LANES)
    x = jnp.stack([jnp.roll(x[:, :, b], 16 * b, axis=-1) for b in range(N_SEQ)], axis=2)
    x = x.reshape(g, nseq * HEAD_DIM, LANES)
    even = ((jnp.arange(LANES) // 8) % 2 == 0)
    return jnp.where(even, x, 0.0), jnp.where(even, 0.0, x)


def _from_columns(y, n_tok, nseq):
    g = y.shape[0]
    x = y.reshape(g, nseq // N_SEQ, N_SEQ, HEAD_DIM, LANES)
    x = jnp.stack([jnp.roll(x[:, :, b], -16 * b, axis=-1) for b in range(N_SEQ)], axis=2)
    x = x.reshape(g, nseq // N_SEQ, N_SEQ, HEAD_DIM, GROUP, LANES // GROUP)[:, :, :, :, :n_tok]
    x = x.transpose(0, 4, 1, 2, 5, 3)
    return x.reshape(g * n_tok, nseq, (LANES // GROUP) * HEAD_DIM)


def _reduce01(fn, x):
    return fn(fn(x, axis=1, keepdims=True), axis=0, keepdims=True)


def _route(x, wt, bias):
    n_exp = wt.shape[0]
    per = n_exp // N_GROUPS
    logits = _dot_nt(wt, x)
    scores = jax.nn.sigmoid(logits)
    sel = (scores + bias).reshape(N_GROUPS, per, ROWS)
    sc3 = scores.reshape(N_GROUPS, per, ROWS)
    neg = -jnp.inf
    pidx = lax.broadcasted_iota(jnp.int32, sel.shape, 1)
    m1 = jnp.max(sel, axis=1, keepdims=True)
    i1 = jnp.min(jnp.where(sel == m1, pidx, per), axis=1, keepdims=True)
    m2 = jnp.max(jnp.where(pidx == i1, neg, sel), axis=1, keepdims=True)
    gscore = m1 + m2
    gidx = lax.broadcasted_iota(jnp.int32, gscore.shape, 0)
    gkeep = jnp.zeros(gscore.shape, F32)
    for _ in range(TOPK_GROUPS):
        gm = jnp.max(gscore, axis=0, keepdims=True)
        gi = jnp.min(jnp.where(gscore == gm, gidx, N_GROUPS), axis=0, keepdims=True)
        hit = gidx == gi
        gkeep = jnp.where(hit, 1.0, gkeep)
        gscore = jnp.where(hit, neg, gscore)
    cand = jnp.where(gkeep > 0.0, sel, neg)
    eidx = lax.broadcasted_iota(jnp.int32, sel.shape, 0) * per + pidx
    gates = jnp.zeros(sel.shape, F32)
    for _ in range(TOP_K):
        em = _reduce01(jnp.max, cand)
        ei = _reduce01(jnp.min, jnp.where(cand == em, eidx, n_exp))
        hit = eidx == ei
        gates = jnp.where(hit, sc3, gates)
        cand = jnp.where(hit, neg, cand)
    total = _reduce01(jnp.sum, gates)
    gates = gates / (total + 1e-20) * ROUTED_SCALE
    return gates.reshape(n_exp, ROWS)


def _transpose_exact(gt, eye):
    p1 = gt.astype(BF16)
    r1 = gt - p1.astype(F32)
    p2 = r1.astype(BF16)
    p3 = (r1 - p2.astype(F32)).astype(BF16)
    return _dot_nt(eye, p1) + _dot_nt(eye, p2) + _dot_nt(eye, p3)


def _post_kernel(xp_ref, xs_ref, cp_ref, cs_ref, yp_ref, ys_ref, gp_ref, gs_ref, bp_ref, bs_ref,
                 p_ref, eye_ref, avg_ref, gng_ref, gnb_ref, wout_ref, ln_g_ref, ln_b_ref, wt_ref, bias_ref,
                 x1p_ref, x1s_ref, gtp_ref, gts_ref, *, alpha):
    i = pl.program_id(0)
    is_s = i == pl.num_programs(0) - 1
    pick = lambda p, s: jnp.where(is_s, s[...], p[...])
    c, y, g, bonus = pick(cp_ref, cs_ref), pick(yp_ref, ys_ref), pick(gp_ref, gs_ref), pick(bp_ref, bs_ref)
    mu = _split_dot(y, avg_ref[...])
    d = y - mu
    var = _split_dot(d * d, avg_ref[...])
    yn = d * lax.rsqrt(var + GN_EPS) * gng_ref[...] + gnb_ref[...]
    o = ((yn + bonus) * g).astype(BF16)
    lhs = jnp.concatenate([c, o], axis=1)
    lhs = jnp.where(is_s, lhs, _dot(p_ref[...], lhs).astype(BF16))
    h = _dot(lhs, wout_ref[...])
    x = jnp.where(is_s, xs_ref[...], xp_ref[...].reshape(ROWS, xp_ref.shape[-1]))
    x1 = _layer_norm(alpha * x + h, ln_g_ref[...], ln_b_ref[...])
    gates = _transpose_exact(_route(x1.astype(BF16), wt_ref[...], bias_ref[...]), eye_ref[...])

    @pl.when(jnp.logical_not(is_s))
    def _():
        x1p_ref[...] = x1.reshape(x1p_ref.shape)
        gtp_ref[...] = gates.reshape(gtp_ref.shape)

    @pl.when(is_s)
    def _():
        x1s_ref[...] = x1
        gts_ref[...] = gates


def _post(xp, xs, cp, cs, yp, ys, gp, gs, bp, bs, consts, alpha):
    nb, t, d = xp.shape
    n_tiles = t // GROUP
    ns = xs.shape[0]
    dr = yp.shape[1]
    n_exp = consts[-2].shape[0]
    pm = lambda w: pl.BlockSpec((ROWS, w), lambda i: (jnp.minimum(i, n_tiles - 1), 0))
    sm = lambda w: pl.BlockSpec((ROWS, w), lambda i: (0, 0))
    tile3 = lambda w: pl.BlockSpec((nb, GROUP, w), lambda i: (0, jnp.minimum(i, n_tiles - 1), 0))
    return pl.pallas_call(
        functools.partial(_post_kernel, alpha=alpha),
        grid=(n_tiles + 1,),
        in_specs=[tile3(d), sm(d), pm(cp.shape[1]), sm(cp.shape[1]), pm(dr), sm(dr), pm(dr), sm(dr), pm(dr), sm(dr)]
                 + [_full(a.shape) for a in consts],
        out_specs=[tile3(d), sm(d), tile3(n_exp), sm(n_exp)],
        out_shape=[jax.ShapeDtypeStruct((nb, t, d), F32), jax.ShapeDtypeStruct((ns, d), F32),
                   jax.ShapeDtypeStruct((nb, t, n_exp), F32), jax.ShapeDtypeStruct((ns, n_exp), F32)],
        compiler_params=_params("arbitrary"),
        name="post",
    )(xp, xs, cp, cs, yp, ys, gp, gs, bp, bs, *consts)


def _experts_kernel(xp_ref, xs_ref, gp_ref, gs_ref, wg_ref, wu_ref, wd_ref, sg_ref, su_ref, sd_ref,
                    ln_g_ref, ln_b_ref, yp_ref, ys_ref, xb_scr, g_scr, *, alpha, n_valid):
    i, e = pl.program_id(0), pl.program_id(1)
    last_i = pl.num_programs(0) - 1
    eps = wg_ref.shape[0]

    @pl.when(e == 0)
    def _():
        @pl.when(i < last_i)
        def _():
            xb_scr[...] = xp_ref[...].astype(BF16)
            g_scr[...] = gp_ref[...]

        @pl.when(i == last_i)
        def _():
            xb_scr[:n_valid, :] = xp_ref[:n_valid, :].astype(BF16)
            xb_scr[n_valid:, :] = xs_ref[...].astype(BF16)
            g_scr[:n_valid, :] = gp_ref[:n_valid, :]
            g_scr[n_valid:, :] = gs_ref[...]

        x0 = xb_scr[...]
        hs = _silu(_dot(x0, sg_ref[...])) * _dot(x0, su_ref[...])
        yp_ref[...] = _dot(hs.astype(BF16), sd_ref[...])

    x = xb_scr[...]
    lane = lax.broadcasted_iota(jnp.int32, g_scr.shape, 1)
    hcat = []
    for q in range(eps):
        h = _silu(_dot(x, wg_ref[q].astype(BF16))) * _dot(x, wu_ref[q].astype(BF16))
        gcol = jnp.sum(jnp.where(lane == e * eps + q, g_scr[...], 0.0), axis=1, keepdims=True)
        hcat.append((h * gcol).astype(BF16))
    wd = wd_ref[...].reshape(eps * wd_ref.shape[1], wd_ref.shape[2]).astype(BF16)
    yp_ref[...] += _dot(jnp.concatenate(hcat, axis=1), wd)

    @pl.when(e == pl.num_programs(1) - 1)
    def _():
        ln = lambda xv, fv: _layer_norm(alpha * xv + fv, ln_g_ref[...], ln_b_ref[...])

        @pl.when(i < last_i)
        def _():
            yp_ref[...] = ln(xp_ref[...], yp_ref[...])

        @pl.when(i == last_i)
        def _():
            ys_ref[...] = ln(xs_ref[...], yp_ref[n_valid:, :])
            yp_ref[:n_valid, :] = ln(xp_ref[:n_valid, :], yp_ref[:n_valid, :])


def _experts(x1p, x1s, gp, gs, wg, wu, wd, sg, su, sd, ln_g, ln_b, alpha):
    n_p, d = x1p.shape
    ns = x1s.shape[0]
    n = n_p + ns
    n_exp, _, de = wg.shape
    eps = 2
    rows = max(r for r in range(16, min(n, MOE_ROWS) + 1, 16) if n % r == 0)
    n_tiles = n // rows
    n_valid = n_p - (n_tiles - 1) * rows
    assert n_exp % eps == 0 and n_valid + ns == rows and n_valid % 16 == 0
    tile = lambda w: pl.BlockSpec((rows, w), lambda i, e: (i, 0))
    whole = lambda a: pl.BlockSpec(a.shape, lambda i, e: (0,) * a.ndim)
    wspec = lambda a: pl.BlockSpec((eps,) + a.shape[1:], lambda i, e: (e, 0, 0))
    return pl.pallas_call(
        functools.partial(_experts_kernel, alpha=alpha, n_valid=n_valid),
        grid=(n_tiles, n_exp // eps),
        in_specs=[tile(d), whole(x1s), tile(n_exp), whole(gs), wspec(wg), wspec(wu), wspec(wd),
                  whole(sg), whole(su), whole(sd), whole(ln_g), whole(ln_b)],
        out_specs=[tile(d), whole(x1s)],
        out_shape=[jax.ShapeDtypeStruct((n_p, d), F32), jax.ShapeDtypeStruct((ns, d), F32)],
        scratch_shapes=[pltpu.VMEM((rows, d), BF16), pltpu.VMEM((rows, n_exp), F32)],
        compiler_params=pltpu.CompilerParams(dimension_semantics=("arbitrary", "arbitrary"),
                                             vmem_limit_bytes=MOE_VMEM_LIMIT),
        name="experts",
    )(x1p, x1s, gp, gs, wg, wu, wd, sg, su, sd, ln_g, ln_b)


def _tile_permutation():
    p = np.zeros((ROWS, ROWS), np.float32)
    for b in range(N_SEQ):
        for t in range(GROUP):
            p[b * GROUP + t, t * N_SEQ + b] = 1.0
    return p


def _row(a):
    return a.reshape(1, -1).astype(F32)


def _layer(xp, xs, conv_s, shift_s, wkv_s, lp, alpha):
    nb, t, d = xp.shape
    ns = xs.shape[0]
    assert nb == N_SEQ and ns == ROWS and t % WKV_TOKENS == 0
    n_p = nb * t
    dc = lp["conv_w"].shape[1]
    dr = lp["w0"].shape[0]
    n_heads = dr // HEAD_DIM
    lw, la, lg = lp["w_up"].shape[0], lp["a_up"].shape[0], lp["g_up"].shape[0]
    lora = lw + la + lg
    lora_pad = -(-lora // LANES) * LANES

    w_in = lp["w_in"]
    wcv = w_in[:, :2 * dc].astype(BF16)
    wrkv = w_in[:, 2 * dc:2 * dc + 3 * dr].astype(BF16)
    wlo = jnp.pad(w_in[:, 2 * dc + 3 * dr:], ((0, 0), (0, lora_pad - lora))).astype(BF16)
    mu = lp["mu_shift"]
    mur = _row(mu[:3 * dr])
    mul = _row(jnp.pad(mu[3 * dr:], (0, lora_pad - lora)))
    wup = jnp.pad(lp["w_up"], ((0, lora_pad - lw), (0, 0))).astype(BF16)
    aup = jnp.pad(lp["a_up"], ((lw, lora_pad - lw - la), (0, 0))).astype(BF16)
    gup = jnp.pad(lp["g_up"], ((lw + la, lora_pad - lora), (0, 0))).astype(BF16)
    head_of = np.arange(dr) // HEAD_DIM
    same_head = (head_of[:, None] == head_of[None, :]).astype(np.float32)
    ones_bd = jnp.asarray(same_head, BF16)
    avg_bd = jnp.asarray(same_head / HEAD_DIM, BF16)
    perm = _tile_permutation()
    prm = [mur, mul, _row(lp["w0"]), _row(lp["a0"]), _row(lp["k_k"]), _row(lp["k_a"]), _row(lp["r_k"]),
           wup, aup, gup, ones_bd]
    conv_prm = [lp["conv_w"].astype(F32), _row(lp["conv_b"]), _row(lp["conv_ln_g"]), _row(lp["conv_ln_b"])]

    u, zr, zl = _in_proj(xp, xs.reshape(ns, d), jnp.asarray(perm.T, BF16), wcv, wrkv, wlo)

    c_p = _conv_seq(u, n_p, *conv_prm)
    c_s = _conv_batch(conv_s.transpose(1, 0, 2), u, n_p, *conv_prm)

    shift_r = shift_s[:, :3 * dr]
    shift_l = jnp.pad(shift_s[:, 3 * dr:], ((0, 0), (0, lora_pad - lora)))
    pre_p = _rwkv_pre(zr, zl, n_p, 0, SEQ_TILE_ROWS, prm)
    pre_s = _rwkv_pre(zr, zl, ns, n_p, ROWS, prm, state=(shift_r, shift_l))

    win = _window_masks()
    hm = jnp.asarray(head_of[None, :] == np.arange(n_heads)[:, None], F32)

    def run_wkv(pre, nseq, tt, tokens, s0):
        r, w, k, a, b, v = [z.reshape(tt, nseq, dr) for z in pre[:6]]
        n_tok = min(GROUP, tokens)
        ve, vo = _to_columns(v, n_tok)
        ycol, s_new = _wkv(r, w, k, a, b, ve, vo, win, hm, s0, tokens)
        return _from_columns(ycol, n_tok, nseq).reshape(tt * nseq, dr), s_new

    y_p, st_p = run_wkv(pre_p, nb, t, WKV_TOKENS, jnp.zeros((nb * HEAD_DIM, dr), F32))
    s0_s = wkv_s.transpose(0, 2, 1, 3).reshape(ns * HEAD_DIM, dr)
    y_s, st_s = run_wkv(pre_s, ns, 1, 1, s0_s)

    n_exp = lp["router_w"].shape[1]
    bias_b = jnp.broadcast_to(lp["router_bias"].astype(F32)[:, None], (n_exp, ROWS))
    post_consts = [jnp.asarray(perm, BF16), jnp.asarray(np.eye(ROWS), BF16), avg_bd, _row(lp["gn_g"]),
                   _row(lp["gn_b"]), lp["w_out"].astype(BF16), _row(lp["ln1_g"]), _row(lp["ln1_b"]),
                   lp["router_w"].T.astype(BF16), bias_b]
    x1p, x1s, gates_p, gates_s = _post(xp, xs.reshape(ns, d), c_p, c_s, y_p, y_s, pre_p[6], pre_s[6],
                                       pre_p[7], pre_s[7], post_consts, alpha)

    y_prompt, y_sample = _experts(x1p.reshape(n_p, d), x1s, gates_p.reshape(n_p, n_exp), gates_s,
                                  lp["exp_gate"], lp["exp_up"], lp["exp_down"], lp["sh_gate"].astype(BF16),
                                  lp["sh_up"].astype(BF16), lp["sh_down"].astype(BF16),
                                  _row(lp["ln2_g"]), _row(lp["ln2_b"]), alpha)
    y_prompt = y_prompt.reshape(nb, t, d)

    taps = lp["conv_w"].shape[0]
    new_conv_p = u[n_p - (taps - 1) * nb:n_p].reshape(taps - 1, nb, dc).transpose(1, 0, 2)
    new_conv_s = jnp.concatenate([conv_s[:, 1:], u[n_p:, None, :]], axis=1)
    z_last = jnp.concatenate([zr, zl[:, :lora]], axis=1)
    new_shift_p = z_last[n_p - nb:n_p]
    new_shift_s = z_last[n_p:]
    unstate = lambda s, nseq: s.reshape(nseq, HEAD_DIM, n_heads, HEAD_DIM).transpose(0, 2, 1, 3)
    return (y_prompt, y_sample.reshape(ns, 1, d), new_conv_p, new_shift_p, unstate(st_p, nb),
            new_conv_s, new_shift_s, unstate(st_s, ns))


def kernel(x_prompt, x_sample, state_conv, state_shift, state_wkv, w_in, mu_shift, conv_w, conv_b, conv_ln_g, conv_ln_b, w0, w_up, a0, a_up, g_up, k_k, k_a, r_k, gn_g, gn_b, w_out, ln1_g, ln1_b, router_w, router_bias, exp_gate, exp_up, exp_down, sh_gate, sh_up, sh_down, ln2_g, ln2_b):
    depth = w_in.shape[0]
    alpha = (2.0 * depth) ** 0.25
    names = ("w_in", "mu_shift", "conv_w", "conv_b", "conv_ln_g", "conv_ln_b", "w0", "w_up", "a0", "a_up", "g_up",
             "k_k", "k_a", "r_k", "gn_g", "gn_b", "w_out", "ln1_g", "ln1_b", "router_w", "router_bias", "exp_gate",
             "exp_up", "exp_down", "sh_gate", "sh_up", "sh_down", "ln2_g", "ln2_b")
    vals = (w_in, mu_shift, conv_w, conv_b, conv_ln_g, conv_ln_b, w0, w_up, a0, a_up, g_up, k_k, k_a, r_k, gn_g,
            gn_b, w_out, ln1_g, ln1_b, router_w, router_bias, exp_gate, exp_up, exp_down, sh_gate, sh_up, sh_down,
            ln2_g, ln2_b)
    xp, xs = x_prompt, x_sample
    outs = [[] for _ in range(6)]
    for l in range(depth):
        lp = {n: v[l] for n, v in zip(names, vals)}
        assert xs.shape[1] == 1
        xp, xs, *states = _layer(xp, xs, state_conv[l], state_shift[l], state_wkv[l], lp, alpha)
        for o, s in zip(outs, states):
            o.append(s)
    return (xp, xs) + tuple(jnp.stack(o) for o in outs)
```

```python
import functools

import jax
import jax.numpy as jnp
import numpy as np
from jax import lax
from jax.experimental import pallas as pl
from jax.experimental.pallas import tpu as pltpu

F32 = jnp.float32
BF16 = jnp.bfloat16

HEAD_DIM = 64
N_SEQ = 8
GROUP = 16
ROWS = GROUP * N_SEQ
LANES = 128
LN_EPS = 1e-5
GN_EPS = 64e-5
ROUTED_SCALE = 2.5
N_GROUPS = 8
TOPK_GROUPS = 4
TOP_K = 8
SEQ_TILE_ROWS = 512
WKV_TOKENS = 64
MOE_ROWS = 1376
VMEM_LIMIT = 48 * 1024 * 1024
MOE_VMEM_LIMIT = 56 * 1024 * 1024


def _params(*sem):
    return pltpu.CompilerParams(dimension_semantics=sem, vmem_limit_bytes=VMEM_LIMIT)


def _full(shape):
    n = len(shape)
    return pl.BlockSpec(shape, lambda *_: (0,) * n)


def _dot(a, b):
    return jnp.dot(a, b, preferred_element_type=F32)


def _dot_nt(a, b):
    return lax.dot_general(a, b, (((1,), (1,)), ((), ())), preferred_element_type=F32)


def _split_dot(x, m):
    hi = x.astype(BF16)
    lo = (x - hi.astype(F32)).astype(BF16)
    return _dot(hi, m) + _dot(lo, m)


def _layer_norm(x, g, b):
    mu = jnp.mean(x, axis=-1, keepdims=True)
    d = x - mu
    var = jnp.mean(d * d, axis=-1, keepdims=True)
    return d * lax.rsqrt(var + LN_EPS) * g + b


def _silu(x):
    return x * jax.nn.sigmoid(x)


def _in_proj_kernel(xp_ref, xs_ref, pt_ref, wcv_ref, wrkv_ref, wlo_ref, u_ref, zr_ref, zl_ref):
    i = pl.program_id(0)
    last = pl.num_programs(0) - 1
    xb = xp_ref[...].reshape(ROWS, xp_ref.shape[-1]).astype(BF16)
    xt = _dot(pt_ref[...], xb).astype(BF16)
    x = jnp.where(i == last, xs_ref[...].astype(BF16), xt)
    zc = _dot(x, wcv_ref[...])
    dc = zc.shape[1] // 2
    u_ref[...] = zc[:, :dc] * jax.nn.sigmoid(zc[:, dc:])
    zr_ref[...] = _dot(x, wrkv_ref[...])
    zl_ref[...] = _dot(x, wlo_ref[...])


def _in_proj(xp, xs, pt, wcv, wrkv, wlo):
    nb, t, d = xp.shape
    n_tiles = t // GROUP
    n = n_tiles * ROWS + xs.shape[0]
    dc, dr, dl = wcv.shape[1] // 2, wrkv.shape[1], wlo.shape[1]
    row = lambda w: pl.BlockSpec((ROWS, w), lambda i: (i, 0))
    return pl.pallas_call(
        _in_proj_kernel,
        grid=(n_tiles + 1,),
        in_specs=[pl.BlockSpec((nb, GROUP, d), lambda i: (0, jnp.minimum(i, n_tiles - 1), 0)),
                  _full(xs.shape), _full(pt.shape), _full(wcv.shape), _full(wrkv.shape), _full(wlo.shape)],
        out_specs=[row(dc), row(dr), row(dl)],
        out_shape=[jax.ShapeDtypeStruct((n, dc), F32), jax.ShapeDtypeStruct((n, dr), F32),
                   jax.ShapeDtypeStruct((n, dl), F32)],
        compiler_params=_params("parallel"),
        name="in_proj",
    )(xp, xs, pt, wcv, wrkv, wlo)


def _conv_tail(acc, g, b):
    return _silu(_layer_norm(acc, g, b)).astype(BF16)


def _conv_seq_kernel(u_ref, w_ref, b_ref, g_ref, beta_ref, c_ref, hist_ref, *, taps, chunk):
    rows = u_ref.shape[0]
    hr = (taps - 1) * N_SEQ

    @pl.when(pl.program_id(0) == 0)
    def _():
        hist_ref[0:hr, :] = jnp.zeros((hr, hist_ref.shape[1]), F32)

    hist_ref[hr:hr + rows, :] = u_ref[...]

    def body(ci, carry):
        r0 = pl.multiple_of(ci * chunk, chunk)
        acc = jnp.broadcast_to(b_ref[...], (chunk, b_ref.shape[1]))
        for j in range(taps):
            acc = acc + w_ref[j:j + 1, :] * hist_ref[pl.ds(pl.multiple_of(r0 + j * N_SEQ, N_SEQ), chunk), :]
        c_ref[pl.ds(r0, chunk), :] = _conv_tail(acc, g_ref[...], beta_ref[...])
        return carry

    lax.fori_loop(0, rows // chunk, body, 0)
    hist_ref[0:hr, :] = hist_ref[rows:rows + hr, :]


def _conv_seq(u, n_rows, w, b, g, beta):
    taps, dc = w.shape
    rows = SEQ_TILE_ROWS
    hr = (taps - 1) * N_SEQ
    assert rows >= hr and n_rows % rows == 0
    return pl.pallas_call(
        functools.partial(_conv_seq_kernel, taps=taps, chunk=64),
        grid=(n_rows // rows,),
        in_specs=[pl.BlockSpec((rows, dc), lambda i: (i, 0)), _full(w.shape), _full(b.shape),
                  _full(g.shape), _full(beta.shape)],
        out_specs=pl.BlockSpec((rows, dc), lambda i: (i, 0)),
        out_shape=jax.ShapeDtypeStruct((n_rows, dc), BF16),
        scratch_shapes=[pltpu.VMEM((hr + rows, dc), F32)],
        compiler_params=_params("arbitrary"),
        name="conv_seq",
    )(u, w, b, g, beta)


def _conv_batch_kernel(cs_ref, u_ref, w_ref, b_ref, g_ref, beta_ref, c_ref, *, taps):
    acc = b_ref[...] + w_ref[taps - 1:taps, :] * u_ref[...]
    for j in range(taps - 1):
        acc = acc + w_ref[j:j + 1, :] * cs_ref[j]
    c_ref[...] = _conv_tail(acc, g_ref[...], beta_ref[...])


def _conv_batch(conv_t, u, row0, w, b, g, beta):
    taps, dc = w.shape
    ns = conv_t.shape[1]
    bs = 32
    assert ns % bs == 0 and row0 % bs == 0
    return pl.pallas_call(
        functools.partial(_conv_batch_kernel, taps=taps),
        grid=(ns // bs,),
        in_specs=[pl.BlockSpec((taps - 1, bs, dc), lambda i: (0, i, 0)),
                  pl.BlockSpec((bs, dc), lambda i: (row0 // bs + i, 0)),
                  _full(w.shape), _full(b.shape), _full(g.shape), _full(beta.shape)],
        out_specs=pl.BlockSpec((bs, dc), lambda i: (i, 0)),
        out_shape=jax.ShapeDtypeStruct((ns, dc), BF16),
        compiler_params=_params("parallel"),
        name="conv_batch",
    )(conv_t, u, w, b, g, beta)


def _rwkv_pre_math(zr, zl, zr_prev, zl_prev, mur, mul, w0, a0, kk_w, ka_w, rk_w, wup, aup, gup, ones_bd):
    dr = w0.shape[1]
    zsr = zr + (zr_prev - zr) * mur
    zsl = zl + (zl_prev - zl) * mul
    r, k, v = zsr[:, :dr], zsr[:, dr:2 * dr], zsr[:, 2 * dr:]
    lw = w0 + _dot(jnp.tanh(zsl).astype(BF16), wup)
    w_log = -(jnp.maximum(-lw, 0.0) + jnp.log(1.0 + jnp.exp(-jnp.abs(lw)))) - 0.5
    decay = jnp.exp(-jnp.exp(w_log))
    a = jax.nn.sigmoid(a0 + _dot(zsl.astype(BF16), aup))
    g = _dot(jax.nn.sigmoid(zsl).astype(BF16), gup)
    kk = k * kk_w
    ss = _split_dot(kk * kk, ones_bd)
    kk = kk / jnp.maximum(jnp.sqrt(ss), 1e-12)
    k2 = k * (1.0 + (a - 1.0) * ka_w)
    bonus = _split_dot(r * k2 * rk_w, ones_bd) * v
    return r, decay, k2, -kk, kk * a, v, g, bonus


N_PRE_PARAMS = 11
N_PRE_OUTS = 8


V_OUT = 5


def _rwkv_pre_seq_kernel(zr_ref, zl_ref, *refs):
    prm, q_ref = refs[:N_PRE_PARAMS], refs[N_PRE_PARAMS]
    outs = refs[N_PRE_PARAMS + 1:N_PRE_PARAMS + 1 + N_PRE_OUTS]
    car_r, car_l = refs[N_PRE_PARAMS + 1 + N_PRE_OUTS:]

    @pl.when(pl.program_id(0) == 0)
    def _():
        car_r[...] = jnp.zeros(car_r.shape, F32)
        car_l[...] = jnp.zeros(car_l.shape, F32)

    zr, zl = zr_ref[...], zl_ref[...]
    zr_prev = jnp.concatenate([car_r[...], zr[:-N_SEQ]], axis=0)
    zl_prev = jnp.concatenate([car_l[...], zl[:-N_SEQ]], axis=0)
    res = _rwkv_pre_math(zr, zl, zr_prev, zl_prev, *[p[...] for p in prm])
    for n, (o, val) in enumerate(zip(outs, res)):
        if n != V_OUT:
            o[...] = val
    v, vc_ref = res[V_OUT], outs[V_OUT]
    n_heads = v.shape[1] // HEAD_DIM
    for gi in range(v.shape[0] // ROWS):
        t1 = v[gi * ROWS:(gi + 1) * ROWS, :].T
        tcat = jnp.concatenate([t1[h * HEAD_DIM:(h + 1) * HEAD_DIM, :] for h in range(n_heads)], axis=1)
        tcat = tcat.astype(BF16)
        for b in range(N_SEQ):
            vc_ref[gi, b * HEAD_DIM:(b + 1) * HEAD_DIM, :] = _dot(tcat, q_ref[b]).astype(BF16)
    car_r[...] = zr[-N_SEQ:]
    car_l[...] = zl[-N_SEQ:]


def _rwkv_pre_batch_kernel(zr_ref, zl_ref, sr_ref, sl_ref, *refs):
    prm, outs = refs[:N_PRE_PARAMS], refs[N_PRE_PARAMS:]
    res = _rwkv_pre_math(zr_ref[...], zl_ref[...], sr_ref[...], sl_ref[...], *[p[...] for p in prm])
    for o, val in zip(outs, res):
        o[...] = val


def _column_selectors(n_heads):
    q = np.zeros((N_SEQ, n_heads * ROWS, 2 * LANES), np.float32)
    for b in range(N_SEQ):
        for h in range(n_heads):
            for j in range(GROUP):
                q[b, h * ROWS + j * N_SEQ + b, (j % 2) * LANES + (16 * b + 8 * j + h) % LANES] = 1.0
    return q


def _rwkv_pre(zr, zl, n_rows, row0, rows, prm, state=None):
    dr3, dl = zr.shape[1], zl.shape[1]
    dr = dr3 // 3
    assert n_rows % rows == 0 and row0 % rows == 0 and rows % ROWS == 0
    zspec = lambda w: pl.BlockSpec((rows, w), lambda i: (row0 // rows + i, 0))
    ospec = pl.BlockSpec((rows, dr), lambda i: (i, 0))
    common = dict(
        grid=(n_rows // rows,),
        out_specs=[ospec] * N_PRE_OUTS,
        out_shape=[jax.ShapeDtypeStruct((n_rows, dr), F32)] * N_PRE_OUTS,
    )
    assert len(prm) == N_PRE_PARAMS
    pspecs = [_full(p.shape) for p in prm]
    if state is None:
        q = jnp.asarray(_column_selectors(dr // HEAD_DIM), BF16)
        srows = N_SEQ * HEAD_DIM
        common["out_specs"][V_OUT] = pl.BlockSpec((rows // ROWS, srows, 2 * LANES), lambda i: (i, 0, 0))
        common["out_shape"][V_OUT] = jax.ShapeDtypeStruct((n_rows // ROWS, srows, 2 * LANES), BF16)
        return pl.pallas_call(
            _rwkv_pre_seq_kernel,
            in_specs=[zspec(dr3), zspec(dl)] + pspecs + [_full(q.shape)],
            scratch_shapes=[pltpu.VMEM((N_SEQ, dr3), F32), pltpu.VMEM((N_SEQ, dl), F32)],
            compiler_params=_params("arbitrary"),
            name="rwkv_pre_seq",
            **common,
        )(zr, zl, *prm, q)
    sr, sl = state
    sspec = lambda w: pl.BlockSpec((rows, w), lambda i: (i, 0))
    return pl.pallas_call(
        _rwkv_pre_batch_kernel,
        in_specs=[zspec(dr3), zspec(dl), sspec(dr3), sspec(dl)] + pspecs,
        compiler_params=_params("parallel"),
        name="rwkv_pre_batch",
        **common,
    )(zr, zl, sr, sl, *prm)


def _wkv_kernel(r_ref, w_ref, k_ref, a_ref, b_ref, vc_ref, win_ref, hm_ref, s0_ref,
                y_ref, st_ref, s_scr, *, n_tok):
    n_groups = vc_ref.shape[0]
    hm = hm_ref[...]
    zeros8 = jnp.zeros(hm.shape, F32)

    @pl.when(pl.program_id(1) == 0)
    def _():
        s_scr[...] = s0_ref[...]

    def masked_rows(x8, b):
        return x8[b:b + 1, :] * hm

    def group(gi, carry):
        yacc = jnp.zeros((N_SEQ * HEAD_DIM, LANES), F32)
        for j in range(n_tok + 1):
            t = gi * n_tok + j
            blocks = [zeros8] * 32
            if j < n_tok:
                a8 = a_ref[t]
                for b in range(N_SEQ):
                    blocks[((16 * b + 8 * j + 8) % LANES) // 8] = masked_rows(a8, b)
            if j > 0:
                r8 = r_ref[t - 1]
                for b in range(N_SEQ):
                    blocks[16 + ((16 * b + 8 * (j - 1)) % LANES) // 8] = masked_rows(r8, b)
            rt = jnp.concatenate(blocks, axis=0).astype(BF16)
            c = _dot_nt(s_scr[...].astype(BF16), rt)
            if j > 0:
                yacc = yacc + c[:, LANES:] * win_ref[j - 1]
            if j < n_tok:
                half = slice(0, LANES) if j % 2 == 0 else slice(LANES, 2 * LANES)
                vcol = vc_ref[gi, :, half].astype(F32)
                lhs2 = ((c[:, :LANES] + vcol) * win_ref[j]).astype(BF16)
                b8, k8, w8 = b_ref[t], k_ref[t], w_ref[t]
                blocks2 = [zeros8] * 16
                for b in range(N_SEQ):
                    blocks2[((16 * b + 8 * j) % LANES) // 8] = masked_rows(k8, b)
                    blocks2[((16 * b + 8 * j + 8) % LANES) // 8] = masked_rows(b8, b)
                rhs2 = jnp.concatenate(blocks2, axis=0).astype(BF16)
                d = _dot(lhs2, rhs2)
                for b in range(N_SEQ):
                    sl = slice(b * HEAD_DIM, (b + 1) * HEAD_DIM)
                    s_scr[sl, :] = s_scr[sl, :] * w8[b:b + 1, :] + d[sl, :]
        y_ref[gi] = yacc
        return carry

    lax.fori_loop(0, n_groups, group, 0)

    @pl.when(pl.program_id(1) == pl.num_programs(1) - 1)
    def _():
        st_ref[...] = s_scr[...]


def _wkv(r, w, k, a, b, vc, win, hm, s0, tokens):
    t, nseq, dr = r.shape
    n_tok = min(GROUP, tokens)
    assert t % tokens == 0 and tokens % n_tok == 0 and nseq % N_SEQ == 0
    assert dr // HEAD_DIM == LANES // GROUP
    gps = tokens // n_tok
    srows = N_SEQ * HEAD_DIM
    tspec = pl.BlockSpec((tokens, N_SEQ, dr), lambda bb, c: (c, bb, 0))
    vspec = lambda w_: pl.BlockSpec((gps, srows, w_), lambda bb, c: (c, bb, 0))
    sspec = pl.BlockSpec((srows, dr), lambda bb, c: (bb, 0))
    return pl.pallas_call(
        functools.partial(_wkv_kernel, n_tok=n_tok),
        grid=(nseq // N_SEQ, t // tokens),
        in_specs=[tspec] * 5 + [vspec(2 * LANES), _full(win.shape), _full(hm.shape), sspec],
        out_specs=[vspec(LANES), sspec],
        out_shape=[jax.ShapeDtypeStruct(vc.shape[:2] + (LANES,), F32), jax.ShapeDtypeStruct(s0.shape, F32)],
        scratch_shapes=[pltpu.VMEM((srows, dr), F32)],
        compiler_params=_params("parallel", "arbitrary"),
        name="wkv",
    )(r, w, k, a, b, vc, win, hm, s0)


def _window_masks():
    lane = np.arange(LANES)[None, None, :]
    b = (np.arange(N_SEQ * HEAD_DIM) // HEAD_DIM)[None, :, None]
    j = np.arange(GROUP)[:, None, None]
    return jnp.asarray(((lane - 16 * b - 8 * j) % LANES) < 16, F32)


def _to_columns(v, n_tok):
    t, nseq, dr = v.shape
    h = dr // HEAD_DIM
    g = t // n_tok
    x = v.reshape(g, n_tok, nseq // N_SEQ, N_SEQ, h, HEAD_DIM)
    x = jnp.pad(x, ((0, 0), (0, GROUP - n_tok)) + ((0, 0),) * 4)
    x = x.transpose(0, 2, 3, 5, 1, 4).reshape(g, nseq // N_SEQ, N_SEQ, HEAD_DIM, LANES)
    x = jnp.stack([jnp.roll(x[:, :, b], 16 * b, axis=-1) for b in range(N_SEQ)], axis=2)
    x = x.reshape(g, nseq * HEAD_DIM, LANES)
    even = ((jnp.arange(LANES) // 8) % 2 == 0)
    return jnp.concatenate([jnp.where(even, x, 0.0), jnp.where(even, 0.0, x)], axis=-1).astype(BF16)


def _from_columns(y, n_tok, nseq):
    g = y.shape[0]
    x = y.reshape(g, nseq // N_SEQ, N_SEQ, HEAD_DIM, LANES)
    x = jnp.stack([jnp.roll(x[:, :, b], -16 * b, axis=-1) for b in range(N_SEQ)], axis=2)
    x = x.reshape(g, nseq // N_SEQ, N_SEQ, HEAD_DIM, GROUP, LANES // GROUP)[:, :, :, :, :n_tok]
    x = x.transpose(0, 4, 1, 2, 5, 3)
    return x.reshape(g * n_tok, nseq, (LANES // GROUP) * HEAD_DIM)


def _reduce01(fn, x):
    return fn(fn(x, axis=1, keepdims=True), axis=0, keepdims=True)


def _route(x, wt, bias):
    n_exp = wt.shape[0]
    per = n_exp // N_GROUPS
    logits = _dot_nt(wt, x)
    scores = jax.nn.sigmoid(logits)
    sel = (scores + bias).reshape(N_GROUPS, per, ROWS)
    sc3 = scores.reshape(N_GROUPS, per, ROWS)
    neg = -jnp.inf
    pidx = lax.broadcasted_iota(jnp.int32, sel.shape, 1)
    m1 = jnp.max(sel, axis=1, keepdims=True)
    i1 = jnp.min(jnp.where(sel == m1, pidx, per), axis=1, keepdims=True)
    m2 = jnp.max(jnp.where(pidx == i1, neg, sel), axis=1, keepdims=True)
    gscore = m1 + m2
    gidx = lax.broadcasted_iota(jnp.int32, gscore.shape, 0)
    gkeep = jnp.zeros(gscore.shape, F32)
    for _ in range(TOPK_GROUPS):
        gm = jnp.max(gscore, axis=0, keepdims=True)
        gi = jnp.min(jnp.where(gscore == gm, gidx, N_GROUPS), axis=0, keepdims=True)
        hit = gidx == gi
        gkeep = jnp.where(hit, 1.0, gkeep)
        gscore = jnp.where(hit, neg, gscore)
    cand = jnp.where(gkeep > 0.0, sel, neg)
    eidx = lax.broadcasted_iota(jnp.int32, sel.shape, 0) * per + pidx
    gates = jnp.zeros(sel.shape, F32)
    for _ in range(TOP_K):
        em = _reduce01(jnp.max, cand)
        ei = _reduce01(jnp.min, jnp.where(cand == em, eidx, n_exp))
        hit = eidx == ei
        gates = jnp.where(hit, sc3, gates)
        cand = jnp.where(hit, neg, cand)
    total = _reduce01(jnp.sum, gates)
    gates = gates / (total + 1e-20) * ROUTED_SCALE
    return gates.reshape(n_exp, ROWS)


def _transpose_exact(gt, eye):
    p1 = gt.astype(BF16)
    r1 = gt - p1.astype(F32)
    p2 = r1.astype(BF16)
    p3 = (r1 - p2.astype(F32)).astype(BF16)
    return _dot_nt(eye, p1) + _dot_nt(eye, p2) + _dot_nt(eye, p3)


def _post_kernel(xp_ref, xs_ref, cp_ref, cs_ref, yp_ref, ys_ref, gp_ref, gs_ref, bp_ref, bs_ref,
                 p_ref, eye_ref, avg_ref, gng_ref, gnb_ref, wout_ref, ln_g_ref, ln_b_ref, wt_ref, bias_ref,
                 x1p_ref, x1s_ref, gtp_ref, gts_ref, *, alpha):
    i = pl.program_id(0)
    is_s = i == pl.num_programs(0) - 1
    pick = lambda p, s: jnp.where(is_s, s[...], p[...])
    c, y, g, bonus = pick(cp_ref, cs_ref), pick(yp_ref, ys_ref), pick(gp_ref, gs_ref), pick(bp_ref, bs_ref)
    mu = _split_dot(y, avg_ref[...])
    d = y - mu
    var = _split_dot(d * d, avg_ref[...])
    yn = d * lax.rsqrt(var + GN_EPS) * gng_ref[...] + gnb_ref[...]
    o = ((yn + bonus) * g).astype(BF16)
    lhs = jnp.concatenate([c, o], axis=1)
    lhs = jnp.where(is_s, lhs, _dot(p_ref[...], lhs).astype(BF16))
    h = _dot(lhs, wout_ref[...])
    x = jnp.where(is_s, xs_ref[...], xp_ref[...].reshape(ROWS, xp_ref.shape[-1]))
    x1 = _layer_norm(alpha * x + h, ln_g_ref[...], ln_b_ref[...])
    gates = _transpose_exact(_route(x1.astype(BF16), wt_ref[...], bias_ref[...]), eye_ref[...])

    @pl.when(jnp.logical_not(is_s))
    def _():
        x1p_ref[...] = x1.reshape(x1p_ref.shape)
        gtp_ref[...] = gates.reshape(gtp_ref.shape)

    @pl.when(is_s)
    def _():
        x1s_ref[...] = x1
        gts_ref[...] = gates


def _post(xp, xs, cp, cs, yp, ys, gp, gs, bp, bs, consts, alpha):
    nb, t, d = xp.shape
    n_tiles = t // GROUP
    ns = xs.shape[0]
    dr = yp.shape[1]
    n_exp = consts[-2].shape[0]
    pm = lambda w: pl.BlockSpec((ROWS, w), lambda i: (jnp.minimum(i, n_tiles - 1), 0))
    sm = lambda w: pl.BlockSpec((ROWS, w), lambda i: (0, 0))
    tile3 = lambda w: pl.BlockSpec((nb, GROUP, w), lambda i: (0, jnp.minimum(i, n_tiles - 1), 0))
    return pl.pallas_call(
        functools.partial(_post_kernel, alpha=alpha),
        grid=(n_tiles + 1,),
        in_specs=[tile3(d), sm(d), pm(cp.shape[1]), sm(cp.shape[1]), pm(dr), sm(dr), pm(dr), sm(dr), pm(dr), sm(dr)]
                 + [_full(a.shape) for a in consts],
        out_specs=[tile3(d), sm(d), tile3(n_exp), sm(n_exp)],
        out_shape=[jax.ShapeDtypeStruct((nb, t, d), F32), jax.ShapeDtypeStruct((ns, d), F32),
                   jax.ShapeDtypeStruct((nb, t, n_exp), F32), jax.ShapeDtypeStruct((ns, n_exp), F32)],
        compiler_params=_params("arbitrary"),
        name="post",
    )(xp, xs, cp, cs, yp, ys, gp, gs, bp, bs, *consts)


def _experts_kernel(xp_ref, xs_ref, gp_ref, gs_ref, wg_ref, wu_ref, wd_ref, sg_ref, su_ref, sd_ref,
                    ln_g_ref, ln_b_ref, yp_ref, ys_ref, xb_scr, g_scr, *, alpha, n_valid):
    i, e = pl.program_id(0), pl.program_id(1)
    last_i = pl.num_programs(0) - 1
    eps = wg_ref.shape[0]

    @pl.when(e == 0)
    def _():
        @pl.when(i < last_i)
        def _():
            xb_scr[...] = xp_ref[...].astype(BF16)
            g_scr[...] = gp_ref[...]

        @pl.when(i == last_i)
        def _():
            xb_scr[:n_valid, :] = xp_ref[:n_valid, :].astype(BF16)
            xb_scr[n_valid:, :] = xs_ref[...].astype(BF16)
            g_scr[:n_valid, :] = gp_ref[:n_valid, :]
            g_scr[n_valid:, :] = gs_ref[...]

        x0 = xb_scr[...]
        hs = _silu(_dot(x0, sg_ref[...])) * _dot(x0, su_ref[...])
        yp_ref[...] = _dot(hs.astype(BF16), sd_ref[...])

    x = xb_scr[...]
    lane = lax.broadcasted_iota(jnp.int32, g_scr.shape, 1)
    hcat = []
    for q in range(eps):
        h = _silu(_dot(x, wg_ref[q].astype(BF16))) * _dot(x, wu_ref[q].astype(BF16))
        gcol = jnp.sum(jnp.where(lane == e * eps + q, g_scr[...], 0.0), axis=1, keepdims=True)
        hcat.append((h * gcol).astype(BF16))
    wd = wd_ref[...].reshape(eps * wd_ref.shape[1], wd_ref.shape[2]).astype(BF16)
    yp_ref[...] += _dot(jnp.concatenate(hcat, axis=1), wd)

    @pl.when(e == pl.num_programs(1) - 1)
    def _():
        ln = lambda xv, fv: _layer_norm(alpha * xv + fv, ln_g_ref[...], ln_b_ref[...])

        @pl.when(i < last_i)
        def _():
            yp_ref[...] = ln(xp_ref[...], yp_ref[...])

        @pl.when(i == last_i)
        def _():
            ys_ref[...] = ln(xs_ref[...], yp_ref[n_valid:, :])
            yp_ref[:n_valid, :] = ln(xp_ref[:n_valid, :], yp_ref[:n_valid, :])


def _experts(x1p, x1s, gp, gs, wg, wu, wd, sg, su, sd, ln_g, ln_b, alpha):
    n_p, d = x1p.shape
    ns = x1s.shape[0]
    n = n_p + ns
    n_exp, _, de = wg.shape
    eps = 2
    rows = max(r for r in range(16, min(n, MOE_ROWS) + 1, 16) if n % r == 0)
    n_tiles = n // rows
    n_valid = n_p - (n_tiles - 1) * rows
    assert n_exp % eps == 0 and n_valid + ns == rows and n_valid % 16 == 0
    tile = lambda w: pl.BlockSpec((rows, w), lambda i, e: (i, 0))
    whole = lambda a: pl.BlockSpec(a.shape, lambda i, e: (0,) * a.ndim)
    wspec = lambda a: pl.BlockSpec((eps,) + a.shape[1:], lambda i, e: (e, 0, 0))
    return pl.pallas_call(
        functools.partial(_experts_kernel, alpha=alpha, n_valid=n_valid),
        grid=(n_tiles, n_exp // eps),
        in_specs=[tile(d), whole(x1s), tile(n_exp), whole(gs), wspec(wg), wspec(wu), wspec(wd),
                  whole(sg), whole(su), whole(sd), whole(ln_g), whole(ln_b)],
        out_specs=[tile(d), whole(x1s)],
        out_shape=[jax.ShapeDtypeStruct((n_p, d), F32), jax.ShapeDtypeStruct((ns, d), F32)],
        scratch_shapes=[pltpu.VMEM((rows, d), BF16), pltpu.VMEM((rows, n_exp), F32)],
        compiler_params=pltpu.CompilerParams(dimension_semantics=("arbitrary", "arbitrary"),
                                             vmem_limit_bytes=MOE_VMEM_LIMIT),
        name="experts",
    )(x1p, x1s, gp, gs, wg, wu, wd, sg, su, sd, ln_g, ln_b)


def _tile_permutation():
    p = np.zeros((ROWS, ROWS), np.float32)
    for b in range(N_SEQ):
        for t in range(GROUP):
            p[b * GROUP + t, t * N_SEQ + b] = 1.0
    return p


def _row(a):
    return a.reshape(1, -1).astype(F32)


def _layer(xp, xs, conv_s, shift_s, wkv_s, lp, alpha):
    nb, t, d = xp.shape
    ns = xs.shape[0]
    assert nb == N_SEQ and ns == ROWS and t % WKV_TOKENS == 0
    n_p = nb * t
    dc = lp["conv_w"].shape[1]
    dr = lp["w0"].shape[0]
    n_heads = dr // HEAD_DIM
    lw, la, lg = lp["w_up"].shape[0], lp["a_up"].shape[0], lp["g_up"].shape[0]
    lora = lw + la + lg
    lora_pad = -(-lora // LANES) * LANES

    w_in = lp["w_in"]
    wcv = w_in[:, :2 * dc].astype(BF16)
    wrkv = w_in[:, 2 * dc:2 * dc + 3 * dr].astype(BF16)
    wlo = jnp.pad(w_in[:, 2 * dc + 3 * dr:], ((0, 0), (0, lora_pad - lora))).astype(BF16)
    mu = lp["mu_shift"]
    mur = _row(mu[:3 * dr])
    mul = _row(jnp.pad(mu[3 * dr:], (0, lora_pad - lora)))
    wup = jnp.pad(lp["w_up"], ((0, lora_pad - lw), (0, 0))).astype(BF16)
    aup = jnp.pad(lp["a_up"], ((lw, lora_pad - lw - la), (0, 0))).astype(BF16)
    gup = jnp.pad(lp["g_up"], ((lw + la, lora_pad - lora), (0, 0))).astype(BF16)
    head_of = np.arange(dr) // HEAD_DIM
    same_head = (head_of[:, None] == head_of[None, :]).astype(np.float32)
    ones_bd = jnp.asarray(same_head, BF16)
    avg_bd = jnp.asarray(same_head / HEAD_DIM, BF16)
    perm = _tile_permutation()
    prm = [mur, mul, _row(lp["w0"]), _row(lp["a0"]), _row(lp["k_k"]), _row(lp["k_a"]), _row(lp["r_k"]),
           wup, aup, gup, ones_bd]
    conv_prm = [lp["conv_w"].astype(F32), _row(lp["conv_b"]), _row(lp["conv_ln_g"]), _row(lp["conv_ln_b"])]

    u, zr, zl = _in_proj(xp, xs.reshape(ns, d), jnp.asarray(perm.T, BF16), wcv, wrkv, wlo)

    c_p = _conv_seq(u, n_p, *conv_prm)
    c_s = _conv_batch(conv_s.transpose(1, 0, 2), u, n_p, *conv_prm)

    shift_r = shift_s[:, :3 * dr]
    shift_l = jnp.pad(shift_s[:, 3 * dr:], ((0, 0), (0, lora_pad - lora)))
    pre_p = _rwkv_pre(zr, zl, n_p, 0, SEQ_TILE_ROWS, prm)
    pre_s = _rwkv_pre(zr, zl, ns, n_p, ROWS, prm, state=(shift_r, shift_l))

    win = _window_masks()
    hm = jnp.asarray(head_of[None, :] == np.arange(n_heads)[:, None], F32)

    def run_wkv(pre, vc, nseq, tt, tokens, s0):
        r, w, k, a, b = [z.reshape(tt, nseq, dr) for z in pre[:5]]
        ycol, s_new = _wkv(r, w, k, a, b, vc, win, hm, s0, tokens)
        return _from_columns(ycol, min(GROUP, tokens), nseq).reshape(tt * nseq, dr), s_new

    y_p, st_p = run_wkv(pre_p, pre_p[V_OUT], nb, t, WKV_TOKENS, jnp.zeros((nb * HEAD_DIM, dr), F32))
    s0_s = wkv_s.transpose(0, 2, 1, 3).reshape(ns * HEAD_DIM, dr)
    y_s, st_s = run_wkv(pre_s, _to_columns(pre_s[V_OUT].reshape(1, ns, dr), 1), ns, 1, 1, s0_s)

    n_exp = lp["router_w"].shape[1]
    bias_b = jnp.broadcast_to(lp["router_bias"].astype(F32)[:, None], (n_exp, ROWS))
    post_consts = [jnp.asarray(perm, BF16), jnp.asarray(np.eye(ROWS), BF16), avg_bd, _row(lp["gn_g"]),
                   _row(lp["gn_b"]), lp["w_out"].astype(BF16), _row(lp["ln1_g"]), _row(lp["ln1_b"]),
                   lp["router_w"].T.astype(BF16), bias_b]
    x1p, x1s, gates_p, gates_s = _post(xp, xs.reshape(ns, d), c_p, c_s, y_p, y_s, pre_p[6], pre_s[6],
                                       pre_p[7], pre_s[7], post_consts, alpha)

    y_prompt, y_sample = _experts(x1p.reshape(n_p, d), x1s, gates_p.reshape(n_p, n_exp), gates_s,
                                  lp["exp_gate"], lp["exp_up"], lp["exp_down"], lp["sh_gate"].astype(BF16),
                                  lp["sh_up"].astype(BF16), lp["sh_down"].astype(BF16),
                                  _row(lp["ln2_g"]), _row(lp["ln2_b"]), alpha)
    y_prompt = y_prompt.reshape(nb, t, d)

    taps = lp["conv_w"].shape[0]
    new_conv_p = u[n_p - (taps - 1) * nb:n_p].reshape(taps - 1, nb, dc).transpose(1, 0, 2)
    new_conv_s = jnp.concatenate([conv_s[:, 1:], u[n_p:, None, :]], axis=1)
    z_rows = lambda lo, hi: jnp.concatenate([zr[lo:hi], zl[lo:hi, :lora]], axis=1)
    new_shift_p = z_rows(n_p - nb, n_p)
    new_shift_s = z_rows(n_p, n_p + ns)
    unstate = lambda s, nseq: s.reshape(nseq, HEAD_DIM, n_heads, HEAD_DIM).transpose(0, 2, 1, 3)
    return (y_prompt, y_sample.reshape(ns, 1, d), new_conv_p, new_shift_p, unstate(st_p, nb),
            new_conv_s, new_shift_s, unstate(st_s, ns))


def kernel(x_prompt, x_sample, state_conv, state_shift, state_wkv, w_in, mu_shift, conv_w, conv_b, conv_ln_g, conv_ln_b, w0, w_up, a0, a_up, g_up, k_k, k_a, r_k, gn_g, gn_b, w_out, ln1_g, ln1_b, router_w, router_bias, exp_gate, exp_up, exp_down, sh_gate, sh_up, sh_down, ln2_g, ln2_b):
    depth = w_in.shape[0]
    alpha = (2.0 * depth) ** 0.25
    names = ("w_in", "mu_shift", "conv_w", "conv_b", "conv_ln_g", "conv_ln_b", "w0", "w_up", "a0", "a_up", "g_up",
             "k_k", "k_a", "r_k", "gn_g", "gn_b", "w_out", "ln1_g", "ln1_b", "router_w", "router_bias", "exp_gate",
             "exp_up", "exp_down", "sh_gate", "sh_up", "sh_down", "ln2_g", "ln2_b")
    vals = (w_in, mu_shift, conv_w, conv_b, conv_ln_g, conv_ln_b, w0, w_up, a0, a_up, g_up, k_k, k_a, r_k, gn_g,
            gn_b, w_out, ln1_g, ln1_b, router_w, router_bias, exp_gate, exp_up, exp_down, sh_gate, sh_up, sh_down,
            ln2_g, ln2_b)
    xp, xs = x_prompt, x_sample
    outs = [[] for _ in range(6)]
    for l in range(depth):
        lp = {n: v[l] for n, v in zip(names, vals)}
        assert xs.shape[1] == 1
        xp, xs, *states = _layer(xp, xs, state_conv[l], state_shift[l], state_wkv[l], lp, alpha)
        for o, s in zip(outs, states):
            o.append(s)
    return (xp, xs) + tuple(jnp.stack(o) for o in outs)
```

```python
import functools

import jax
import jax.numpy as jnp
import numpy as np
from jax import lax
from jax.experimental import pallas as pl
from jax.experimental.pallas import tpu as pltpu

F32 = jnp.float32
BF16 = jnp.bfloat16

HEAD_DIM = 64
N_SEQ = 8
GROUP = 16
ROWS = GROUP * N_SEQ
LANES = 128
LN_EPS = 1e-5
GN_EPS = 64e-5
ROUTED_SCALE = 2.5
N_GROUPS = 8
TOPK_GROUPS = 4
TOP_K = 8
SEQ_TILE_ROWS = 512
WKV_TOKENS = 64
WKV_CHAINS = 2
MOE_ROWS = 1376
VMEM_LIMIT = 48 * 1024 * 1024
MOE_VMEM_LIMIT = 56 * 1024 * 1024


def _params(*sem):
    return pltpu.CompilerParams(dimension_semantics=sem, vmem_limit_bytes=VMEM_LIMIT)


def _full(shape):
    n = len(shape)
    return pl.BlockSpec(shape, lambda *_: (0,) * n)


def _dot(a, b):
    return jnp.dot(a, b, preferred_element_type=F32)


def _dot_nt(a, b):
    return lax.dot_general(a, b, (((1,), (1,)), ((), ())), preferred_element_type=F32)


def _split_dot(x, m):
    hi = x.astype(BF16)
    lo = (x - hi.astype(F32)).astype(BF16)
    return _dot(hi, m) + _dot(lo, m)


def _layer_norm(x, g, b):
    mu = jnp.mean(x, axis=-1, keepdims=True)
    d = x - mu
    var = jnp.mean(d * d, axis=-1, keepdims=True)
    return d * lax.rsqrt(var + LN_EPS) * g + b


def _silu(x):
    return x * jax.nn.sigmoid(x)


def _in_proj_kernel(xp_ref, xs_ref, pt_ref, wcv_ref, wrkv_ref, wlo_ref, u_ref, zr_ref, zl_ref):
    i = pl.program_id(0)
    last = pl.num_programs(0) - 1
    xb = xp_ref[...].reshape(ROWS, xp_ref.shape[-1]).astype(BF16)
    xt = _dot(pt_ref[...], xb).astype(BF16)
    x = jnp.where(i == last, xs_ref[...].astype(BF16), xt)
    zc = _dot(x, wcv_ref[...])
    dc = zc.shape[1] // 2
    u_ref[...] = zc[:, :dc] * jax.nn.sigmoid(zc[:, dc:])
    zr_ref[...] = _dot(x, wrkv_ref[...])
    zl_ref[...] = _dot(x, wlo_ref[...])


def _in_proj(xp, xs, pt, wcv, wrkv, wlo):
    nb, t, d = xp.shape
    n_tiles = t // GROUP
    n = n_tiles * ROWS + xs.shape[0]
    dc, dr, dl = wcv.shape[1] // 2, wrkv.shape[1], wlo.shape[1]
    row = lambda w: pl.BlockSpec((ROWS, w), lambda i: (i, 0))
    return pl.pallas_call(
        _in_proj_kernel,
        grid=(n_tiles + 1,),
        in_specs=[pl.BlockSpec((nb, GROUP, d), lambda i: (0, jnp.minimum(i, n_tiles - 1), 0)),
                  _full(xs.shape), _full(pt.shape), _full(wcv.shape), _full(wrkv.shape), _full(wlo.shape)],
        out_specs=[row(dc), row(dr), row(dl)],
        out_shape=[jax.ShapeDtypeStruct((n, dc), F32), jax.ShapeDtypeStruct((n, dr), F32),
                   jax.ShapeDtypeStruct((n, dl), F32)],
        compiler_params=_params("parallel"),
        name="in_proj",
    )(xp, xs, pt, wcv, wrkv, wlo)


def _conv_tail(acc, g, b):
    return _silu(_layer_norm(acc, g, b)).astype(BF16)


def _conv_seq_kernel(u_ref, w_ref, b_ref, g_ref, beta_ref, c_ref, hist_ref, *, taps, chunk):
    rows = u_ref.shape[0]
    hr = (taps - 1) * N_SEQ

    @pl.when(pl.program_id(0) == 0)
    def _():
        hist_ref[0:hr, :] = jnp.zeros((hr, hist_ref.shape[1]), F32)

    hist_ref[hr:hr + rows, :] = u_ref[...]

    def body(ci, carry):
        r0 = pl.multiple_of(ci * chunk, chunk)
        dc = b_ref.shape[1]
        acc = jnp.broadcast_to(b_ref[...], (chunk, dc)).reshape(chunk // N_SEQ, N_SEQ, dc)
        for j in range(taps):
            win = hist_ref[pl.ds(pl.multiple_of(r0 + j * N_SEQ, N_SEQ), chunk), :]
            acc = acc + w_ref[j] * win.reshape(chunk // N_SEQ, N_SEQ, dc)
        c_ref[pl.ds(r0, chunk), :] = _conv_tail(acc.reshape(chunk, dc), g_ref[...], beta_ref[...])
        return carry

    lax.fori_loop(0, rows // chunk, body, 0)
    hist_ref[0:hr, :] = hist_ref[rows:rows + hr, :]


def _conv_seq(u, n_rows, w, b, g, beta):
    taps, dc = w.shape
    rows = SEQ_TILE_ROWS
    hr = (taps - 1) * N_SEQ
    assert rows >= hr and n_rows % rows == 0
    w = jnp.broadcast_to(w[:, None, :], (taps, N_SEQ, dc))
    return pl.pallas_call(
        functools.partial(_conv_seq_kernel, taps=taps, chunk=64),
        grid=(n_rows // rows,),
        in_specs=[pl.BlockSpec((rows, dc), lambda i: (i, 0)), _full(w.shape), _full(b.shape),
                  _full(g.shape), _full(beta.shape)],
        out_specs=pl.BlockSpec((rows, dc), lambda i: (i, 0)),
        out_shape=jax.ShapeDtypeStruct((n_rows, dc), BF16),
        scratch_shapes=[pltpu.VMEM((hr + rows, dc), F32)],
        compiler_params=_params("arbitrary"),
        name="conv_seq",
    )(u, w, b, g, beta)


def _conv_batch_kernel(cs_ref, u_ref, w_ref, b_ref, g_ref, beta_ref, c_ref, *, taps):
    acc = b_ref[...] + w_ref[taps - 1:taps, :] * u_ref[...]
    for j in range(taps - 1):
        acc = acc + w_ref[j:j + 1, :] * cs_ref[j]
    c_ref[...] = _conv_tail(acc, g_ref[...], beta_ref[...])


def _conv_batch(conv_t, u, row0, w, b, g, beta):
    taps, dc = w.shape
    ns = conv_t.shape[1]
    bs = 32
    assert ns % bs == 0 and row0 % bs == 0
    return pl.pallas_call(
        functools.partial(_conv_batch_kernel, taps=taps),
        grid=(ns // bs,),
        in_specs=[pl.BlockSpec((taps - 1, bs, dc), lambda i: (0, i, 0)),
                  pl.BlockSpec((bs, dc), lambda i: (row0 // bs + i, 0)),
                  _full(w.shape), _full(b.shape), _full(g.shape), _full(beta.shape)],
        out_specs=pl.BlockSpec((bs, dc), lambda i: (i, 0)),
        out_shape=jax.ShapeDtypeStruct((ns, dc), BF16),
        compiler_params=_params("parallel"),
        name="conv_batch",
    )(conv_t, u, w, b, g, beta)


def _rwkv_pre_math(zr, zl, zr_prev, zl_prev, mur, mul, w0, a0, kk_w, ka_w, rk_w, wup, aup, gup, ones_bd):
    dr = w0.shape[1]
    zsr = zr + (zr_prev - zr) * mur
    zsl = zl + (zl_prev - zl) * mul
    r, k, v = zsr[:, :dr], zsr[:, dr:2 * dr], zsr[:, 2 * dr:]
    lw = w0 + _dot(jnp.tanh(zsl).astype(BF16), wup)
    w_log = -(jnp.maximum(-lw, 0.0) + jnp.log(1.0 + jnp.exp(-jnp.abs(lw)))) - 0.5
    decay = jnp.exp(-jnp.exp(w_log))
    a = jax.nn.sigmoid(a0 + _dot(zsl.astype(BF16), aup))
    g = _dot(jax.nn.sigmoid(zsl).astype(BF16), gup)
    kk = k * kk_w
    ss = _split_dot(kk * kk, ones_bd)
    kk = kk / jnp.maximum(jnp.sqrt(ss), 1e-12)
    k2 = k * (1.0 + (a - 1.0) * ka_w)
    bonus = _split_dot(r * k2 * rk_w, ones_bd) * v
    return r, decay, k2, -kk, kk * a, v, g, bonus


N_PRE_PARAMS = 11
N_PRE_OUTS = 8


V_OUT = 5


def _rwkv_pre_seq_kernel(zr_ref, zl_ref, *refs):
    prm, q_ref = refs[:N_PRE_PARAMS], refs[N_PRE_PARAMS]
    outs = refs[N_PRE_PARAMS + 1:N_PRE_PARAMS + 1 + N_PRE_OUTS]
    car_r, car_l = refs[N_PRE_PARAMS + 1 + N_PRE_OUTS:]

    @pl.when(pl.program_id(0) == 0)
    def _():
        car_r[...] = jnp.zeros(car_r.shape, F32)
        car_l[...] = jnp.zeros(car_l.shape, F32)

    zr, zl = zr_ref[...], zl_ref[...]
    zr_prev = jnp.concatenate([car_r[...], zr[:-N_SEQ]], axis=0)
    zl_prev = jnp.concatenate([car_l[...], zl[:-N_SEQ]], axis=0)
    res = _rwkv_pre_math(zr, zl, zr_prev, zl_prev, *[p[...] for p in prm])
    for n, (o, val) in enumerate(zip(outs, res)):
        if n != V_OUT:
            o[...] = val
    v, vc_ref = res[V_OUT], outs[V_OUT]
    n_heads = v.shape[1] // HEAD_DIM
    n_groups = v.shape[0] // ROWS
    tcats = []
    for gi in range(n_groups):
        t1 = v[gi * ROWS:(gi + 1) * ROWS, :].T
        tcat = jnp.concatenate([t1[h * HEAD_DIM:(h + 1) * HEAD_DIM, :] for h in range(n_heads)], axis=1)
        tcats.append(tcat.astype(BF16))
    tcat_all = jnp.concatenate(tcats, axis=0)
    for b in range(N_SEQ):
        cols = _dot(tcat_all, q_ref[b]).astype(BF16)
        for gi in range(n_groups):
            vc_ref[gi, b * HEAD_DIM:(b + 1) * HEAD_DIM, :] = cols[gi * HEAD_DIM:(gi + 1) * HEAD_DIM, :]
    car_r[...] = zr[-N_SEQ:]
    car_l[...] = zl[-N_SEQ:]


def _rwkv_pre_batch_kernel(zr_ref, zl_ref, sr_ref, sl_ref, *refs):
    prm, outs = refs[:N_PRE_PARAMS], refs[N_PRE_PARAMS:]
    res = _rwkv_pre_math(zr_ref[...], zl_ref[...], sr_ref[...], sl_ref[...], *[p[...] for p in prm])
    for o, val in zip(outs, res):
        o[...] = val


def _column_selectors(n_heads):
    q = np.zeros((N_SEQ, n_heads * ROWS, 2 * LANES), np.float32)
    for b in range(N_SEQ):
        for h in range(n_heads):
            for j in range(GROUP):
                q[b, h * ROWS + j * N_SEQ + b, (j % 2) * LANES + (16 * b + 8 * j + h) % LANES] = 1.0
    return q


def _rwkv_pre(zr, zl, n_rows, row0, rows, prm, state=None):
    dr3, dl = zr.shape[1], zl.shape[1]
    dr = dr3 // 3
    assert n_rows % rows == 0 and row0 % rows == 0 and rows % ROWS == 0
    zspec = lambda w: pl.BlockSpec((rows, w), lambda i: (row0 // rows + i, 0))
    ospec = pl.BlockSpec((rows, dr), lambda i: (i, 0))
    common = dict(
        grid=(n_rows // rows,),
        out_specs=[ospec] * N_PRE_OUTS,
        out_shape=[jax.ShapeDtypeStruct((n_rows, dr), F32)] * N_PRE_OUTS,
    )
    assert len(prm) == N_PRE_PARAMS
    pspecs = [_full(p.shape) for p in prm]
    if state is None:
        q = jnp.asarray(_column_selectors(dr // HEAD_DIM), BF16)
        srows = N_SEQ * HEAD_DIM
        common["out_specs"][V_OUT] = pl.BlockSpec((rows // ROWS, srows, 2 * LANES), lambda i: (i, 0, 0))
        common["out_shape"][V_OUT] = jax.ShapeDtypeStruct((n_rows // ROWS, srows, 2 * LANES), BF16)
        return pl.pallas_call(
            _rwkv_pre_seq_kernel,
            in_specs=[zspec(dr3), zspec(dl)] + pspecs + [_full(q.shape)],
            scratch_shapes=[pltpu.VMEM((N_SEQ, dr3), F32), pltpu.VMEM((N_SEQ, dl), F32)],
            compiler_params=_params("arbitrary"),
            name="rwkv_pre_seq",
            **common,
        )(zr, zl, *prm, q)
    sr, sl = state
    sspec = lambda w: pl.BlockSpec((rows, w), lambda i: (i, 0))
    return pl.pallas_call(
        _rwkv_pre_batch_kernel,
        in_specs=[zspec(dr3), zspec(dl), sspec(dr3), sspec(dl)] + pspecs,
        compiler_params=_params("parallel"),
        name="rwkv_pre_batch",
        **common,
    )(zr, zl, sr, sl, *prm)


def _wkv_kernel(r_ref, w_ref, k_ref, a_ref, b_ref, vc_ref, win_ref, hm_ref, s0_ref,
                y_ref, st_ref, s_scr, *, n_tok):
    n_groups = vc_ref.shape[0]
    hm = hm_ref[...]
    zeros8 = jnp.zeros(hm.shape, F32)

    @pl.when(pl.program_id(1) == 0)
    def _():
        s_scr[...] = s0_ref[...]

    def masked_rows(x8, b):
        return x8[b:b + 1, :] * hm

    def group(gi, carry):
        seq_per_chain = N_SEQ // WKV_CHAINS
        chain_rows = seq_per_chain * HEAD_DIM
        yacc = [jnp.zeros((chain_rows, LANES), F32)] * WKV_CHAINS
        for j in range(n_tok + 1):
            t = gi * n_tok + j
            blocks = [zeros8] * 32
            if j < n_tok:
                a8 = a_ref[t]
                for b in range(N_SEQ):
                    blocks[((16 * b + 8 * j + 8) % LANES) // 8] = masked_rows(a8, b)
            if j > 0:
                r8 = r_ref[t - 1]
                for b in range(N_SEQ):
                    blocks[16 + ((16 * b + 8 * (j - 1)) % LANES) // 8] = masked_rows(r8, b)
            rt = jnp.concatenate(blocks, axis=0).astype(BF16)
            if j < n_tok:
                b8, k8, w8 = b_ref[t], k_ref[t], w_ref[t]
                blocks2 = [zeros8] * 16
                for b in range(N_SEQ):
                    blocks2[((16 * b + 8 * j) % LANES) // 8] = masked_rows(k8, b)
                    blocks2[((16 * b + 8 * j + 8) % LANES) // 8] = masked_rows(b8, b)
                rhs2 = jnp.concatenate(blocks2, axis=0).astype(BF16)
                half = slice(0, LANES) if j % 2 == 0 else slice(LANES, 2 * LANES)
            for ch in range(WKV_CHAINS):
                rows = slice(ch * chain_rows, (ch + 1) * chain_rows)
                c = _dot_nt(s_scr[rows, :].astype(BF16), rt)
                if j > 0:
                    yacc[ch] = yacc[ch] + c[:, LANES:] * win_ref[j - 1, rows, :]
                if j < n_tok:
                    vcol = vc_ref[gi, rows, half].astype(F32)
                    lhs2 = ((c[:, :LANES] + vcol) * win_ref[j, rows, :]).astype(BF16)
                    d = _dot(lhs2, rhs2)
                    for b in range(ch * seq_per_chain, (ch + 1) * seq_per_chain):
                        sl = slice(b * HEAD_DIM, (b + 1) * HEAD_DIM)
                        dl = slice(sl.start - rows.start, sl.stop - rows.start)
                        s_scr[sl, :] = s_scr[sl, :] * w8[b:b + 1, :] + d[dl, :]
        for ch in range(WKV_CHAINS):
            y_ref[gi, ch * chain_rows:(ch + 1) * chain_rows, :] = yacc[ch]
        return carry

    lax.fori_loop(0, n_groups, group, 0)

    @pl.when(pl.program_id(1) == pl.num_programs(1) - 1)
    def _():
        st_ref[...] = s_scr[...]


def _wkv(r, w, k, a, b, vc, win, hm, s0, tokens):
    t, nseq, dr = r.shape
    n_tok = min(GROUP, tokens)
    assert t % tokens == 0 and tokens % n_tok == 0 and nseq % N_SEQ == 0
    assert dr // HEAD_DIM == LANES // GROUP
    gps = tokens // n_tok
    srows = N_SEQ * HEAD_DIM
    tspec = pl.BlockSpec((tokens, N_SEQ, dr), lambda bb, c: (c, bb, 0))
    vspec = lambda w_: pl.BlockSpec((gps, srows, w_), lambda bb, c: (c, bb, 0))
    sspec = pl.BlockSpec((srows, dr), lambda bb, c: (bb, 0))
    return pl.pallas_call(
        functools.partial(_wkv_kernel, n_tok=n_tok),
        grid=(nseq // N_SEQ, t // tokens),
        in_specs=[tspec] * 5 + [vspec(2 * LANES), _full(win.shape), _full(hm.shape), sspec],
        out_specs=[vspec(LANES), sspec],
        out_shape=[jax.ShapeDtypeStruct(vc.shape[:2] + (LANES,), F32), jax.ShapeDtypeStruct(s0.shape, F32)],
        scratch_shapes=[pltpu.VMEM((srows, dr), F32)],
        compiler_params=_params("parallel", "arbitrary"),
        name="wkv",
    )(r, w, k, a, b, vc, win, hm, s0)


def _window_masks():
    lane = np.arange(LANES)[None, None, :]
    b = (np.arange(N_SEQ * HEAD_DIM) // HEAD_DIM)[None, :, None]
    j = np.arange(GROUP)[:, None, None]
    return jnp.asarray(((lane - 16 * b - 8 * j) % LANES) < 16, F32)


def _to_columns(v, n_tok):
    t, nseq, dr = v.shape
    h = dr // HEAD_DIM
    g = t // n_tok
    x = v.reshape(g, n_tok, nseq // N_SEQ, N_SEQ, h, HEAD_DIM)
    x = jnp.pad(x, ((0, 0), (0, GROUP - n_tok)) + ((0, 0),) * 4)
    x = x.transpose(0, 2, 3, 5, 1, 4).reshape(g, nseq // N_SEQ, N_SEQ, HEAD_DIM, LANES)
    x = jnp.stack([jnp.roll(x[:, :, b], 16 * b, axis=-1) for b in range(N_SEQ)], axis=2)
    x = x.reshape(g, nseq * HEAD_DIM, LANES)
    even = ((jnp.arange(LANES) // 8) % 2 == 0)
    return jnp.concatenate([jnp.where(even, x, 0.0), jnp.where(even, 0.0, x)], axis=-1).astype(BF16)


def _from_columns(y, n_tok, nseq):
    g = y.shape[0]
    x = y.reshape(g, nseq // N_SEQ, N_SEQ, HEAD_DIM, LANES)
    x = jnp.stack([jnp.roll(x[:, :, b], -16 * b, axis=-1) for b in range(N_SEQ)], axis=2)
    x = x.reshape(g, nseq // N_SEQ, N_SEQ, HEAD_DIM, GROUP, LANES // GROUP)[:, :, :, :, :n_tok]
    x = x.transpose(0, 4, 1, 2, 5, 3)
    return x.reshape(g * n_tok, nseq, (LANES // GROUP) * HEAD_DIM)


def _reduce01(fn, x):
    return fn(fn(x, axis=1, keepdims=True), axis=0, keepdims=True)


def _route(x, wt, bias):
    n_exp = wt.shape[0]
    per = n_exp // N_GROUPS
    logits = _dot_nt(wt, x)
    scores = jax.nn.sigmoid(logits)
    sel = (scores + bias).reshape(N_GROUPS, per, ROWS)
    sc3 = scores.reshape(N_GROUPS, per, ROWS)
    neg = -jnp.inf
    pidx = lax.broadcasted_iota(jnp.int32, sel.shape, 1)
    m1 = jnp.max(sel, axis=1, keepdims=True)
    i1 = jnp.min(jnp.where(sel == m1, pidx, per), axis=1, keepdims=True)
    m2 = jnp.max(jnp.where(pidx == i1, neg, sel), axis=1, keepdims=True)
    gscore = m1 + m2
    gidx = lax.broadcasted_iota(jnp.int32, gscore.shape, 0)
    gkeep = jnp.zeros(gscore.shape, F32)
    for _ in range(TOPK_GROUPS):
        gm = jnp.max(gscore, axis=0, keepdims=True)
        gi = jnp.min(jnp.where(gscore == gm, gidx, N_GROUPS), axis=0, keepdims=True)
        hit = gidx == gi
        gkeep = jnp.where(hit, 1.0, gkeep)
        gscore = jnp.where(hit, neg, gscore)
    cand = jnp.where(gkeep > 0.0, sel, neg)
    eidx = lax.broadcasted_iota(jnp.int32, sel.shape, 0) * per + pidx
    gates = jnp.zeros(sel.shape, F32)
    for _ in range(TOP_K):
        em = _reduce01(jnp.max, cand)
        ei = _reduce01(jnp.min, jnp.where(cand == em, eidx, n_exp))
        hit = eidx == ei
        gates = jnp.where(hit, sc3, gates)
        cand = jnp.where(hit, neg, cand)
    total = _reduce01(jnp.sum, gates)
    gates = gates / (total + 1e-20) * ROUTED_SCALE
    return gates.reshape(n_exp, ROWS)


def _transpose_exact(gt, eye):
    p1 = gt.astype(BF16)
    r1 = gt - p1.astype(F32)
    p2 = r1.astype(BF16)
    p3 = (r1 - p2.astype(F32)).astype(BF16)
    return _dot_nt(eye, p1) + _dot_nt(eye, p2) + _dot_nt(eye, p3)


def _post_kernel(xp_ref, xs_ref, cp_ref, cs_ref, yp_ref, ys_ref, gp_ref, gs_ref, bp_ref, bs_ref,
                 p_ref, eye_ref, avg_ref, gng_ref, gnb_ref, wout_ref, ln_g_ref, ln_b_ref, wt_ref, bias_ref,
                 x1p_ref, x1s_ref, gtp_ref, gts_ref, *, alpha):
    i = pl.program_id(0)
    is_s = i == pl.num_programs(0) - 1
    pick = lambda p, s: jnp.where(is_s, s[...], p[...])
    c, y, g, bonus = pick(cp_ref, cs_ref), pick(yp_ref, ys_ref), pick(gp_ref, gs_ref), pick(bp_ref, bs_ref)
    mu = _split_dot(y, avg_ref[...])
    d = y - mu
    var = _split_dot(d * d, avg_ref[...])
    yn = d * lax.rsqrt(var + GN_EPS) * gng_ref[...] + gnb_ref[...]
    o = ((yn + bonus) * g).astype(BF16)
    lhs = jnp.concatenate([c, o], axis=1)
    lhs = jnp.where(is_s, lhs, _dot(p_ref[...], lhs).astype(BF16))
    h = _dot(lhs, wout_ref[...])
    x = jnp.where(is_s, xs_ref[...], xp_ref[...].reshape(ROWS, xp_ref.shape[-1]))
    x1 = _layer_norm(alpha * x + h, ln_g_ref[...], ln_b_ref[...])
    gates = _transpose_exact(_route(x1.astype(BF16), wt_ref[...], bias_ref[...]), eye_ref[...])

    @pl.when(jnp.logical_not(is_s))
    def _():
        x1p_ref[...] = x1.reshape(x1p_ref.shape)
        gtp_ref[...] = gates.reshape(gtp_ref.shape)

    @pl.when(is_s)
    def _():
        x1s_ref[...] = x1
        gts_ref[...] = gates


def _post(xp, xs, cp, cs, yp, ys, gp, gs, bp, bs, consts, alpha):
    nb, t, d = xp.shape
    n_tiles = t // GROUP
    ns = xs.shape[0]
    dr = yp.shape[1]
    n_exp = consts[-2].shape[0]
    pm = lambda w: pl.BlockSpec((ROWS, w), lambda i: (jnp.minimum(i, n_tiles - 1), 0))
    sm = lambda w: pl.BlockSpec((ROWS, w), lambda i: (0, 0))
    tile3 = lambda w: pl.BlockSpec((nb, GROUP, w), lambda i: (0, jnp.minimum(i, n_tiles - 1), 0))
    return pl.pallas_call(
        functools.partial(_post_kernel, alpha=alpha),
        grid=(n_tiles + 1,),
        in_specs=[tile3(d), sm(d), pm(cp.shape[1]), sm(cp.shape[1]), pm(dr), sm(dr), pm(dr), sm(dr), pm(dr), sm(dr)]
                 + [_full(a.shape) for a in consts],
        out_specs=[tile3(d), sm(d), tile3(n_exp), sm(n_exp)],
        out_shape=[jax.ShapeDtypeStruct((nb, t, d), F32), jax.ShapeDtypeStruct((ns, d), F32),
                   jax.ShapeDtypeStruct((nb, t, n_exp), F32), jax.ShapeDtypeStruct((ns, n_exp), F32)],
        compiler_params=_params("arbitrary"),
        name="post",
    )(xp, xs, cp, cs, yp, ys, gp, gs, bp, bs, *consts)


def _experts_kernel(xp_ref, xs_ref, gp_ref, gs_ref, wg_ref, wu_ref, wd_ref, sg_ref, su_ref, sd_ref,
                    ln_g_ref, ln_b_ref, yp_ref, ys_ref, xb_scr, g_scr, *, alpha, n_valid):
    i, e = pl.program_id(0), pl.program_id(1)
    last_i = pl.num_programs(0) - 1
    eps = wg_ref.shape[0]

    @pl.when(e == 0)
    def _():
        @pl.when(i < last_i)
        def _():
            xb_scr[...] = xp_ref[...].astype(BF16)
            g_scr[...] = gp_ref[...]

        @pl.when(i == last_i)
        def _():
            xb_scr[:n_valid, :] = xp_ref[:n_valid, :].astype(BF16)
            xb_scr[n_valid:, :] = xs_ref[...].astype(BF16)
            g_scr[:n_valid, :] = gp_ref[:n_valid, :]
            g_scr[n_valid:, :] = gs_ref[...]

        x0 = xb_scr[...]
        hs = _silu(_dot(x0, sg_ref[...])) * _dot(x0, su_ref[...])
        yp_ref[...] = _dot(hs.astype(BF16), sd_ref[...])

    x = xb_scr[...]
    lane = lax.broadcasted_iota(jnp.int32, g_scr.shape, 1)
    hcat = []
    for q in range(eps):
        h = _silu(_dot(x, wg_ref[q].astype(BF16))) * _dot(x, wu_ref[q].astype(BF16))
        gcol = jnp.sum(jnp.where(lane == e * eps + q, g_scr[...], 0.0), axis=1, keepdims=True)
        hcat.append((h * gcol).astype(BF16))
    wd = wd_ref[...].reshape(eps * wd_ref.shape[1], wd_ref.shape[2]).astype(BF16)
    yp_ref[...] += _dot(jnp.concatenate(hcat, axis=1), wd)

    @pl.when(e == pl.num_programs(1) - 1)
    def _():
        ln = lambda xv, fv: _layer_norm(alpha * xv + fv, ln_g_ref[...], ln_b_ref[...])

        @pl.when(i < last_i)
        def _():
            yp_ref[...] = ln(xp_ref[...], yp_ref[...])

        @pl.when(i == last_i)
        def _():
            ys_ref[...] = ln(xs_ref[...], yp_ref[n_valid:, :])
            yp_ref[:n_valid, :] = ln(xp_ref[:n_valid, :], yp_ref[:n_valid, :])


def _experts(x1p, x1s, gp, gs, wg, wu, wd, sg, su, sd, ln_g, ln_b, alpha):
    n_p, d = x1p.shape
    ns = x1s.shape[0]
    n = n_p + ns
    n_exp, _, de = wg.shape
    eps = 2
    rows = max(r for r in range(16, min(n, MOE_ROWS) + 1, 16) if n % r == 0)
    n_tiles = n // rows
    n_valid = n_p - (n_tiles - 1) * rows
    assert n_exp % eps == 0 and n_valid + ns == rows and n_valid % 16 == 0
    tile = lambda w: pl.BlockSpec((rows, w), lambda i, e: (i, 0))
    whole = lambda a: pl.BlockSpec(a.shape, lambda i, e: (0,) * a.ndim)
    wspec = lambda a: pl.BlockSpec((eps,) + a.shape[1:], lambda i, e: (e, 0, 0))
    return pl.pallas_call(
        functools.partial(_experts_kernel, alpha=alpha, n_valid=n_valid),
        grid=(n_tiles, n_exp // eps),
        in_specs=[tile(d), whole(x1s), tile(n_exp), whole(gs), wspec(wg), wspec(wu), wspec(wd),
                  whole(sg), whole(su), whole(sd), whole(ln_g), whole(ln_b)],
        out_specs=[tile(d), whole(x1s)],
        out_shape=[jax.ShapeDtypeStruct((n_p, d), F32), jax.ShapeDtypeStruct((ns, d), F32)],
        scratch_shapes=[pltpu.VMEM((rows, d), BF16), pltpu.VMEM((rows, n_exp), F32)],
        compiler_params=pltpu.CompilerParams(dimension_semantics=("arbitrary", "arbitrary"),
                                             vmem_limit_bytes=MOE_VMEM_LIMIT),
        name="experts",
    )(x1p, x1s, gp, gs, wg, wu, wd, sg, su, sd, ln_g, ln_b)


def _tile_permutation():
    p = np.zeros((ROWS, ROWS), np.float32)
    for b in range(N_SEQ):
        for t in range(GROUP):
            p[b * GROUP + t, t * N_SEQ + b] = 1.0
    return p


def _row(a):
    return a.reshape(1, -1).astype(F32)


def _layer(xp, xs, conv_s, shift_s, wkv_s, lp, alpha):
    nb, t, d = xp.shape
    ns = xs.shape[0]
    assert nb == N_SEQ and ns == ROWS and t % WKV_TOKENS == 0
    n_p = nb * t
    dc = lp["conv_w"].shape[1]
    dr = lp["w0"].shape[0]
    n_heads = dr // HEAD_DIM
    lw, la, lg = lp["w_up"].shape[0], lp["a_up"].shape[0], lp["g_up"].shape[0]
    lora = lw + la + lg
    lora_pad = -(-lora // LANES) * LANES

    w_in = lp["w_in"]
    wcv = w_in[:, :2 * dc].astype(BF16)
    wrkv = w_in[:, 2 * dc:2 * dc + 3 * dr].astype(BF16)
    wlo = jnp.pad(w_in[:, 2 * dc + 3 * dr:], ((0, 0), (0, lora_pad - lora))).astype(BF16)
    mu = lp["mu_shift"]
    mur = _row(mu[:3 * dr])
    mul = _row(jnp.pad(mu[3 * dr:], (0, lora_pad - lora)))
    wup = jnp.pad(lp["w_up"], ((0, lora_pad - lw), (0, 0))).astype(BF16)
    aup = jnp.pad(lp["a_up"], ((lw, lora_pad - lw - la), (0, 0))).astype(BF16)
    gup = jnp.pad(lp["g_up"], ((lw + la, lora_pad - lora), (0, 0))).astype(BF16)
    head_of = np.arange(dr) // HEAD_DIM
    same_head = (head_of[:, None] == head_of[None, :]).astype(np.float32)
    ones_bd = jnp.asarray(same_head, BF16)
    avg_bd = jnp.asarray(same_head / HEAD_DIM, BF16)
    perm = _tile_permutation()
    prm = [mur, mul, _row(lp["w0"]), _row(lp["a0"]), _row(lp["k_k"]), _row(lp["k_a"]), _row(lp["r_k"]),
           wup, aup, gup, ones_bd]
    conv_prm = [lp["conv_w"].astype(F32), _row(lp["conv_b"]), _row(lp["conv_ln_g"]), _row(lp["conv_ln_b"])]

    u, zr, zl = _in_proj(xp, xs.reshape(ns, d), jnp.asarray(perm.T, BF16), wcv, wrkv, wlo)

    c_p = _conv_seq(u, n_p, *conv_prm)
    c_s = _conv_batch(conv_s.transpose(1, 0, 2), u, n_p, *conv_prm)

    shift_r = shift_s[:, :3 * dr]
    shift_l = jnp.pad(shift_s[:, 3 * dr:], ((0, 0), (0, lora_pad - lora)))
    pre_p = _rwkv_pre(zr, zl, n_p, 0, SEQ_TILE_ROWS, prm)
    pre_s = _rwkv_pre(zr, zl, ns, n_p, ROWS, prm, state=(shift_r, shift_l))

    win = _window_masks()
    hm = jnp.asarray(head_of[None, :] == np.arange(n_heads)[:, None], F32)

    def run_wkv(pre, vc, nseq, tt, tokens, s0):
        r, w, k, a, b = [z.reshape(tt, nseq, dr) for z in pre[:5]]
        ycol, s_new = _wkv(r, w, k, a, b, vc, win, hm, s0, tokens)
        return _from_columns(ycol, min(GROUP, tokens), nseq).reshape(tt * nseq, dr), s_new

    y_p, st_p = run_wkv(pre_p, pre_p[V_OUT], nb, t, WKV_TOKENS, jnp.zeros((nb * HEAD_DIM, dr), F32))
    s0_s = wkv_s.transpose(0, 2, 1, 3).reshape(ns * HEAD_DIM, dr)
    y_s, st_s = run_wkv(pre_s, _to_columns(pre_s[V_OUT].reshape(1, ns, dr), 1), ns, 1, 1, s0_s)

    n_exp = lp["router_w"].shape[1]
    bias_b = jnp.broadcast_to(lp["router_bias"].astype(F32)[:, None], (n_exp, ROWS))
    post_consts = [jnp.asarray(perm, BF16), jnp.asarray(np.eye(ROWS), BF16), avg_bd, _row(lp["gn_g"]),
                   _row(lp["gn_b"]), lp["w_out"].astype(BF16), _row(lp["ln1_g"]), _row(lp["ln1_b"]),
                   lp["router_w"].T.astype(BF16), bias_b]
    x1p, x1s, gates_p, gates_s = _post(xp, xs.reshape(ns, d), c_p, c_s, y_p, y_s, pre_p[6], pre_s[6],
                                       pre_p[7], pre_s[7], post_consts, alpha)

    y_prompt, y_sample = _experts(x1p.reshape(n_p, d), x1s, gates_p.reshape(n_p, n_exp), gates_s,
                                  lp["exp_gate"], lp["exp_up"], lp["exp_down"], lp["sh_gate"].astype(BF16),
                                  lp["sh_up"].astype(BF16), lp["sh_down"].astype(BF16),
                                  _row(lp["ln2_g"]), _row(lp["ln2_b"]), alpha)
    y_prompt = y_prompt.reshape(nb, t, d)

    taps = lp["conv_w"].shape[0]
    new_conv_p = u[n_p - (taps - 1) * nb:n_p].reshape(taps - 1, nb, dc).transpose(1, 0, 2)
    new_conv_s = jnp.concatenate([conv_s[:, 1:], u[n_p:, None, :]], axis=1)
    z_rows = lambda lo, hi: jnp.concatenate([zr[lo:hi], zl[lo:hi, :lora]], axis=1)
    new_shift_p = z_rows(n_p - nb, n_p)
    new_shift_s = z_rows(n_p, n_p + ns)
    unstate = lambda s, nseq: s.reshape(nseq, HEAD_DIM, n_heads, HEAD_DIM).transpose(0, 2, 1, 3)
    return (y_prompt, y_sample.reshape(ns, 1, d), new_conv_p, new_shift_p, unstate(st_p, nb),
            new_conv_s, new_shift_s, unstate(st_s, ns))


def kernel(x_prompt, x_sample, state_conv, state_shift, state_wkv, w_in, mu_shift, conv_w, conv_b, conv_ln_g, conv_ln_b, w0, w_up, a0, a_up, g_up, k_k, k_a, r_k, gn_g, gn_b, w_out, ln1_g, ln1_b, router_w, router_bias, exp_gate, exp_up, exp_down, sh_gate, sh_up, sh_down, ln2_g, ln2_b):
    depth = w_in.shape[0]
    alpha = (2.0 * depth) ** 0.25
    names = ("w_in", "mu_shift", "conv_w", "conv_b", "conv_ln_g", "conv_ln_b", "w0", "w_up", "a0", "a_up", "g_up",
             "k_k", "k_a", "r_k", "gn_g", "gn_b", "w_out", "ln1_g", "ln1_b", "router_w", "router_bias", "exp_gate",
             "exp_up", "exp_down", "sh_gate", "sh_up", "sh_down", "ln2_g", "ln2_b")
    vals = (w_in, mu_shift, conv_w, conv_b, conv_ln_g, conv_ln_b, w0, w_up, a0, a_up, g_up, k_k, k_a, r_k, gn_g,
            gn_b, w_out, ln1_g, ln1_b, router_w, router_bias, exp_gate, exp_up, exp_down, sh_gate, sh_up, sh_down,
            ln2_g, ln2_b)
    xp, xs = x_prompt, x_sample
    outs = [[] for _ in range(6)]
    for l in range(depth):
        lp = {n: v[l] for n, v in zip(names, vals)}
        assert xs.shape[1] == 1
        xp, xs, *states = _layer(xp, xs, state_conv[l], state_shift[l], state_wkv[l], lp, alpha)
        for o, s in zip(outs, states):
            o.append(s)
    return (xp, xs) + tuple(jnp.stack(o) for o in outs)
```

```python
import functools

import jax
import jax.numpy as jnp
import numpy as np
from jax import lax
from jax.experimental import pallas as pl
from jax.experimental.pallas import tpu as pltpu

F32 = jnp.float32
BF16 = jnp.bfloat16

HEAD_DIM = 64
N_SEQ = 8
GROUP = 16
ROWS = GROUP * N_SEQ
LANES = 128
LN_EPS = 1e-5
GN_EPS = 64e-5
ROUTED_SCALE = 2.5
N_GROUPS = 8
TOPK_GROUPS = 4
TOP_K = 8
SEQ_TILE_ROWS = 512
WKV_TOKENS = 64
WKV_CHAINS = 2
MOE_ROWS = 1376
VMEM_LIMIT = 48 * 1024 * 1024
MOE_VMEM_LIMIT = 56 * 1024 * 1024


def _params(*sem):
    return pltpu.CompilerParams(dimension_semantics=sem, vmem_limit_bytes=VMEM_LIMIT)


def _full(shape):
    n = len(shape)
    return pl.BlockSpec(shape, lambda *_: (0,) * n)


def _dot(a, b):
    return jnp.dot(a, b, preferred_element_type=F32)


def _dot_nt(a, b):
    return lax.dot_general(a, b, (((1,), (1,)), ((), ())), preferred_element_type=F32)


def _split_dot(x, m):
    hi = x.astype(BF16)
    lo = (x - hi.astype(F32)).astype(BF16)
    return _dot(hi, m) + _dot(lo, m)


def _layer_norm(x, g, b):
    mu = jnp.mean(x, axis=-1, keepdims=True)
    d = x - mu
    var = jnp.mean(d * d, axis=-1, keepdims=True)
    return d * lax.rsqrt(var + LN_EPS) * g + b


def _silu(x):
    return x * jax.nn.sigmoid(x)


def _in_proj_kernel(xp_ref, xs_ref, pt_ref, wcv_ref, wrkv_ref, wlo_ref, u_ref, zr_ref, zl_ref):
    i = pl.program_id(0)
    last = pl.num_programs(0) - 1
    xb = xp_ref[...].reshape(ROWS, xp_ref.shape[-1]).astype(BF16)
    xt = _dot(pt_ref[...], xb).astype(BF16)
    x = jnp.where(i == last, xs_ref[...].astype(BF16), xt)
    zc = _dot(x, wcv_ref[...])
    dc = zc.shape[1] // 2
    u_ref[...] = zc[:, :dc] * jax.nn.sigmoid(zc[:, dc:])
    zr_ref[...] = _dot(x, wrkv_ref[...])
    zl_ref[...] = _dot(x, wlo_ref[...])


def _in_proj(xp, xs, pt, wcv, wrkv, wlo):
    nb, t, d = xp.shape
    n_tiles = t // GROUP
    n = n_tiles * ROWS + xs.shape[0]
    dc, dr, dl = wcv.shape[1] // 2, wrkv.shape[1], wlo.shape[1]
    row = lambda w: pl.BlockSpec((ROWS, w), lambda i: (i, 0))
    return pl.pallas_call(
        _in_proj_kernel,
        grid=(n_tiles + 1,),
        in_specs=[pl.BlockSpec((nb, GROUP, d), lambda i: (0, jnp.minimum(i, n_tiles - 1), 0)),
                  _full(xs.shape), _full(pt.shape), _full(wcv.shape), _full(wrkv.shape), _full(wlo.shape)],
        out_specs=[row(dc), row(dr), row(dl)],
        out_shape=[jax.ShapeDtypeStruct((n, dc), F32), jax.ShapeDtypeStruct((n, dr), F32),
                   jax.ShapeDtypeStruct((n, dl), F32)],
        compiler_params=_params("parallel"),
        name="in_proj",
    )(xp, xs, pt, wcv, wrkv, wlo)


def _conv_tail(acc, g, b):
    return _silu(_layer_norm(acc, g, b)).astype(BF16)


def _conv_seq_kernel(u_ref, w_ref, b_ref, g_ref, beta_ref, c_ref, hist_ref, *, taps, chunk):
    rows = u_ref.shape[0]
    hr = (taps - 1) * N_SEQ

    @pl.when(pl.program_id(0) == 0)
    def _():
        hist_ref[0:hr, :] = jnp.zeros((hr, hist_ref.shape[1]), F32)

    hist_ref[hr:hr + rows, :] = u_ref[...]

    def body(ci, carry):
        r0 = pl.multiple_of(ci * chunk, chunk)
        dc = b_ref.shape[1]
        acc = jnp.broadcast_to(b_ref[...], (chunk, dc)).reshape(chunk // N_SEQ, N_SEQ, dc)
        for j in range(taps):
            win = hist_ref[pl.ds(pl.multiple_of(r0 + j * N_SEQ, N_SEQ), chunk), :]
            acc = acc + w_ref[j] * win.reshape(chunk // N_SEQ, N_SEQ, dc)
        c_ref[pl.ds(r0, chunk), :] = _conv_tail(acc.reshape(chunk, dc), g_ref[...], beta_ref[...])
        return carry

    lax.fori_loop(0, rows // chunk, body, 0)
    hist_ref[0:hr, :] = hist_ref[rows:rows + hr, :]


def _conv_seq(u, n_rows, w, b, g, beta):
    taps, dc = w.shape
    rows = SEQ_TILE_ROWS
    hr = (taps - 1) * N_SEQ
    assert rows >= hr and n_rows % rows == 0
    w = jnp.broadcast_to(w[:, None, :], (taps, N_SEQ, dc))
    return pl.pallas_call(
        functools.partial(_conv_seq_kernel, taps=taps, chunk=64),
        grid=(n_rows // rows,),
        in_specs=[pl.BlockSpec((rows, dc), lambda i: (i, 0)), _full(w.shape), _full(b.shape),
                  _full(g.shape), _full(beta.shape)],
        out_specs=pl.BlockSpec((rows, dc), lambda i: (i, 0)),
        out_shape=jax.ShapeDtypeStruct((n_rows, dc), BF16),
        scratch_shapes=[pltpu.VMEM((hr + rows, dc), F32)],
        compiler_params=_params("arbitrary"),
        name="conv_seq",
    )(u, w, b, g, beta)


def _conv_batch_kernel(cs_ref, u_ref, w_ref, b_ref, g_ref, beta_ref, c_ref, *, taps):
    acc = b_ref[...] + w_ref[taps - 1:taps, :] * u_ref[...]
    for j in range(taps - 1):
        acc = acc + w_ref[j:j + 1, :] * cs_ref[j]
    c_ref[...] = _conv_tail(acc, g_ref[...], beta_ref[...])


def _conv_batch(conv_t, u, row0, w, b, g, beta):
    taps, dc = w.shape
    ns = conv_t.shape[1]
    bs = 32
    assert ns % bs == 0 and row0 % bs == 0
    return pl.pallas_call(
        functools.partial(_conv_batch_kernel, taps=taps),
        grid=(ns // bs,),
        in_specs=[pl.BlockSpec((taps - 1, bs, dc), lambda i: (0, i, 0)),
                  pl.BlockSpec((bs, dc), lambda i: (row0 // bs + i, 0)),
                  _full(w.shape), _full(b.shape), _full(g.shape), _full(beta.shape)],
        out_specs=pl.BlockSpec((bs, dc), lambda i: (i, 0)),
        out_shape=jax.ShapeDtypeStruct((ns, dc), BF16),
        compiler_params=_params("parallel"),
        name="conv_batch",
    )(conv_t, u, w, b, g, beta)


def _rwkv_pre_math(zr, zl, zr_prev, zl_prev, mur, mul, w0, a0, kk_w, ka_w, rk_w, wup, aup, gup, ones_bd):
    dr = w0.shape[1]
    zsr = zr + (zr_prev - zr) * mur
    zsl = zl + (zl_prev - zl) * mul
    r, k, v = zsr[:, :dr], zsr[:, dr:2 * dr], zsr[:, 2 * dr:]
    lw = w0 + _dot(jnp.tanh(zsl).astype(BF16), wup)
    w_log = -(jnp.maximum(-lw, 0.0) + jnp.log(1.0 + jnp.exp(-jnp.abs(lw)))) - 0.5
    decay = jnp.exp(-jnp.exp(w_log))
    a = jax.nn.sigmoid(a0 + _dot(zsl.astype(BF16), aup))
    g = _dot(jax.nn.sigmoid(zsl).astype(BF16), gup)
    kk = k * kk_w
    ss = _split_dot(kk * kk, ones_bd)
    kk = kk / jnp.maximum(jnp.sqrt(ss), 1e-12)
    k2 = k * (1.0 + (a - 1.0) * ka_w)
    bonus = _split_dot(r * k2 * rk_w, ones_bd) * v
    return r, decay, k2, -kk, kk * a, v, g, bonus


N_PRE_PARAMS = 11
N_PRE_OUTS = 8


V_OUT = 5


def _rwkv_pre_seq_kernel(zr_ref, zl_ref, *refs):
    prm, q_ref = refs[:N_PRE_PARAMS], refs[N_PRE_PARAMS]
    outs = refs[N_PRE_PARAMS + 1:N_PRE_PARAMS + 1 + N_PRE_OUTS]
    car_r, car_l = refs[N_PRE_PARAMS + 1 + N_PRE_OUTS:]

    @pl.when(pl.program_id(0) == 0)
    def _():
        car_r[...] = jnp.zeros(car_r.shape, F32)
        car_l[...] = jnp.zeros(car_l.shape, F32)

    zr, zl = zr_ref[...], zl_ref[...]
    zr_prev = jnp.concatenate([car_r[...], zr[:-N_SEQ]], axis=0)
    zl_prev = jnp.concatenate([car_l[...], zl[:-N_SEQ]], axis=0)
    res = _rwkv_pre_math(zr, zl, zr_prev, zl_prev, *[p[...] for p in prm])
    for n, (o, val) in enumerate(zip(outs, res)):
        if n != V_OUT:
            o[...] = val
    v, vc_ref = res[V_OUT], outs[V_OUT]
    n_heads = v.shape[1] // HEAD_DIM
    n_groups = v.shape[0] // ROWS
    tcats = []
    for gi in range(n_groups):
        t1 = v[gi * ROWS:(gi + 1) * ROWS, :].T
        tcat = jnp.concatenate([t1[h * HEAD_DIM:(h + 1) * HEAD_DIM, :] for h in range(n_heads)], axis=1)
        tcats.append(tcat.astype(BF16))
    tcat_all = jnp.concatenate(tcats, axis=0)
    for b in range(N_SEQ):
        cols = _dot(tcat_all, q_ref[b]).astype(BF16)
        for gi in range(n_groups):
            vc_ref[gi, b * HEAD_DIM:(b + 1) * HEAD_DIM, :] = cols[gi * HEAD_DIM:(gi + 1) * HEAD_DIM, :]
    car_r[...] = zr[-N_SEQ:]
    car_l[...] = zl[-N_SEQ:]


def _rwkv_pre_batch_kernel(zr_ref, zl_ref, sr_ref, sl_ref, *refs):
    prm, outs = refs[:N_PRE_PARAMS], refs[N_PRE_PARAMS:]
    res = _rwkv_pre_math(zr_ref[...], zl_ref[...], sr_ref[...], sl_ref[...], *[p[...] for p in prm])
    for o, val in zip(outs, res):
        o[...] = val


def _column_selectors(n_heads):
    q = np.zeros((N_SEQ, n_heads * ROWS, 2 * LANES), np.float32)
    for b in range(N_SEQ):
        for h in range(n_heads):
            for j in range(GROUP):
                q[b, h * ROWS + j * N_SEQ + b, (j % 2) * LANES + (16 * b + 8 * j + h) % LANES] = 1.0
    return q


def _rwkv_pre(zr, zl, n_rows, row0, rows, prm, state=None):
    dr3, dl = zr.shape[1], zl.shape[1]
    dr = dr3 // 3
    assert n_rows % rows == 0 and row0 % rows == 0 and rows % ROWS == 0
    zspec = lambda w: pl.BlockSpec((rows, w), lambda i: (row0 // rows + i, 0))
    ospec = pl.BlockSpec((rows, dr), lambda i: (i, 0))
    common = dict(
        grid=(n_rows // rows,),
        out_specs=[ospec] * N_PRE_OUTS,
        out_shape=[jax.ShapeDtypeStruct((n_rows, dr), F32)] * N_PRE_OUTS,
    )
    assert len(prm) == N_PRE_PARAMS
    pspecs = [_full(p.shape) for p in prm]
    if state is None:
        q = jnp.asarray(_column_selectors(dr // HEAD_DIM), BF16)
        srows = N_SEQ * HEAD_DIM
        common["out_specs"][V_OUT] = pl.BlockSpec((rows // ROWS, srows, 2 * LANES), lambda i: (i, 0, 0))
        common["out_shape"][V_OUT] = jax.ShapeDtypeStruct((n_rows // ROWS, srows, 2 * LANES), BF16)
        return pl.pallas_call(
            _rwkv_pre_seq_kernel,
            in_specs=[zspec(dr3), zspec(dl)] + pspecs + [_full(q.shape)],
            scratch_shapes=[pltpu.VMEM((N_SEQ, dr3), F32), pltpu.VMEM((N_SEQ, dl), F32)],
            compiler_params=_params("arbitrary"),
            name="rwkv_pre_seq",
            **common,
        )(zr, zl, *prm, q)
    sr, sl = state
    sspec = lambda w: pl.BlockSpec((rows, w), lambda i: (i, 0))
    return pl.pallas_call(
        _rwkv_pre_batch_kernel,
        in_specs=[zspec(dr3), zspec(dl), sspec(dr3), sspec(dl)] + pspecs,
        compiler_params=_params("parallel"),
        name="rwkv_pre_batch",
        **common,
    )(zr, zl, sr, sl, *prm)


def _wkv_kernel(r_ref, w_ref, k_ref, a_ref, b_ref, vc_ref, win_ref, hm_ref, s0_ref,
                y_ref, st_ref, s_scr, *, n_tok):
    n_groups = vc_ref.shape[0]
    hm = hm_ref[...]
    zeros8 = jnp.zeros(hm.shape, F32)

    @pl.when(pl.program_id(1) == 0)
    def _():
        s_scr[...] = s0_ref[...]

    def masked_rows(x8, b):
        return x8[b:b + 1, :] * hm

    def group(gi, carry):
        seq_per_chain = N_SEQ // WKV_CHAINS
        chain_rows = seq_per_chain * HEAD_DIM
        yacc = [jnp.zeros((chain_rows, LANES), F32)] * WKV_CHAINS
        for j in range(n_tok + 1):
            t = gi * n_tok + j
            blocks = [zeros8] * 32
            if j < n_tok:
                a8 = a_ref[t]
                for b in range(N_SEQ):
                    blocks[((16 * b + 8 * j + 8) % LANES) // 8] = masked_rows(a8, b)
            if j > 0:
                r8 = r_ref[t - 1]
                for b in range(N_SEQ):
                    blocks[16 + ((16 * b + 8 * (j - 1)) % LANES) // 8] = masked_rows(r8, b)
            rt = jnp.concatenate(blocks, axis=0).astype(BF16)
            if j < n_tok:
                b8, k8, w8 = b_ref[t], k_ref[t], w_ref[t]
                blocks2 = [zeros8] * 16
                for b in range(N_SEQ):
                    blocks2[((16 * b + 8 * j) % LANES) // 8] = masked_rows(k8, b)
                    blocks2[((16 * b + 8 * j + 8) % LANES) // 8] = masked_rows(b8, b)
                rhs2 = jnp.concatenate(blocks2, axis=0).astype(BF16)
                half = slice(0, LANES) if j % 2 == 0 else slice(LANES, 2 * LANES)
            for ch in range(WKV_CHAINS):
                rows = slice(ch * chain_rows, (ch + 1) * chain_rows)
                c = _dot_nt(s_scr[rows, :].astype(BF16), rt)
                if j > 0:
                    yacc[ch] = yacc[ch] + c[:, LANES:] * win_ref[j - 1, rows, :]
                if j < n_tok:
                    vcol = vc_ref[gi, rows, half].astype(F32)
                    lhs2 = ((c[:, :LANES] + vcol) * win_ref[j, rows, :]).astype(BF16)
                    d = _dot(lhs2, rhs2)
                    for b in range(ch * seq_per_chain, (ch + 1) * seq_per_chain):
                        sl = slice(b * HEAD_DIM, (b + 1) * HEAD_DIM)
                        dl = slice(sl.start - rows.start, sl.stop - rows.start)
                        s_scr[sl, :] = s_scr[sl, :] * w8[b:b + 1, :] + d[dl, :]
        for ch in range(WKV_CHAINS):
            y_ref[gi, ch * chain_rows:(ch + 1) * chain_rows, :] = yacc[ch]
        return carry

    lax.fori_loop(0, n_groups, group, 0)

    @pl.when(pl.program_id(1) == pl.num_programs(1) - 1)
    def _():
        st_ref[...] = s_scr[...]


def _wkv(r, w, k, a, b, vc, win, hm, s0, tokens):
    t, nseq, dr = r.shape
    n_tok = min(GROUP, tokens)
    assert t % tokens == 0 and tokens % n_tok == 0 and nseq % N_SEQ == 0
    assert dr // HEAD_DIM == LANES // GROUP
    gps = tokens // n_tok
    srows = N_SEQ * HEAD_DIM
    tspec = pl.BlockSpec((tokens, N_SEQ, dr), lambda bb, c: (c, bb, 0))
    vspec = lambda w_: pl.BlockSpec((gps, srows, w_), lambda bb, c: (c, bb, 0))
    sspec = pl.BlockSpec((srows, dr), lambda bb, c: (bb, 0))
    return pl.pallas_call(
        functools.partial(_wkv_kernel, n_tok=n_tok),
        grid=(nseq // N_SEQ, t // tokens),
        in_specs=[tspec] * 5 + [vspec(2 * LANES), _full(win.shape), _full(hm.shape), sspec],
        out_specs=[vspec(LANES), sspec],
        out_shape=[jax.ShapeDtypeStruct(vc.shape[:2] + (LANES,), F32), jax.ShapeDtypeStruct(s0.shape, F32)],
        scratch_shapes=[pltpu.VMEM((srows, dr), F32)],
        compiler_params=_params("parallel", "arbitrary"),
        name="wkv",
    )(r, w, k, a, b, vc, win, hm, s0)


def _window_masks():
    lane = np.arange(LANES)[None, None, :]
    b = (np.arange(N_SEQ * HEAD_DIM) // HEAD_DIM)[None, :, None]
    j = np.arange(GROUP)[:, None, None]
    return jnp.asarray(((lane - 16 * b - 8 * j) % LANES) < 16, F32)


def _to_columns(v, n_tok):
    t, nseq, dr = v.shape
    h = dr // HEAD_DIM
    g = t // n_tok
    x = v.reshape(g, n_tok, nseq // N_SEQ, N_SEQ, h, HEAD_DIM)
    x = jnp.pad(x, ((0, 0), (0, GROUP - n_tok)) + ((0, 0),) * 4)
    x = x.transpose(0, 2, 3, 5, 1, 4).reshape(g, nseq // N_SEQ, N_SEQ, HEAD_DIM, LANES)
    x = jnp.stack([jnp.roll(x[:, :, b], 16 * b, axis=-1) for b in range(N_SEQ)], axis=2)
    x = x.reshape(g, nseq * HEAD_DIM, LANES)
    even = ((jnp.arange(LANES) // 8) % 2 == 0)
    return jnp.concatenate([jnp.where(even, x, 0.0), jnp.where(even, 0.0, x)], axis=-1).astype(BF16)


def _from_columns(y, n_tok, nseq, unrotate=True):
    g = y.shape[0]
    x = y.reshape(g, nseq // N_SEQ, N_SEQ, HEAD_DIM, LANES)
    if unrotate:
        x = jnp.stack([jnp.roll(x[:, :, b], -16 * b, axis=-1) for b in range(N_SEQ)], axis=2)
    x = x.reshape(g, nseq // N_SEQ, N_SEQ, HEAD_DIM, GROUP, LANES // GROUP)[:, :, :, :, :n_tok]
    x = x.transpose(0, 4, 1, 2, 5, 3)
    return x.reshape(g * n_tok, nseq, (LANES // GROUP) * HEAD_DIM)


def _reduce01(fn, x):
    return fn(fn(x, axis=1, keepdims=True), axis=0, keepdims=True)


def _route(x, wt, bias):
    n_exp = wt.shape[0]
    per = n_exp // N_GROUPS
    logits = _dot_nt(wt, x)
    scores = jax.nn.sigmoid(logits)
    sel = (scores + bias).reshape(N_GROUPS, per, ROWS)
    sc3 = scores.reshape(N_GROUPS, per, ROWS)
    neg = -jnp.inf
    pidx = lax.broadcasted_iota(jnp.int32, sel.shape, 1)
    m1 = jnp.max(sel, axis=1, keepdims=True)
    i1 = jnp.min(jnp.where(sel == m1, pidx, per), axis=1, keepdims=True)
    m2 = jnp.max(jnp.where(pidx == i1, neg, sel), axis=1, keepdims=True)
    gscore = m1 + m2
    gidx = lax.broadcasted_iota(jnp.int32, gscore.shape, 0)
    gkeep = jnp.zeros(gscore.shape, F32)
    for _ in range(TOPK_GROUPS):
        gm = jnp.max(gscore, axis=0, keepdims=True)
        gi = jnp.min(jnp.where(gscore == gm, gidx, N_GROUPS), axis=0, keepdims=True)
        hit = gidx == gi
        gkeep = jnp.where(hit, 1.0, gkeep)
        gscore = jnp.where(hit, neg, gscore)
    cand = jnp.where(gkeep > 0.0, sel, neg)
    eidx = lax.broadcasted_iota(jnp.int32, sel.shape, 0) * per + pidx
    gates = jnp.zeros(sel.shape, F32)
    for _ in range(TOP_K):
        em = _reduce01(jnp.max, cand)
        ei = _reduce01(jnp.min, jnp.where(cand == em, eidx, n_exp))
        hit = eidx == ei
        gates = jnp.where(hit, sc3, gates)
        cand = jnp.where(hit, neg, cand)
    total = _reduce01(jnp.sum, gates)
    gates = gates / (total + 1e-20) * ROUTED_SCALE
    return gates.reshape(n_exp, ROWS)


def _transpose_exact(gt, eye):
    p1 = gt.astype(BF16)
    r1 = gt - p1.astype(F32)
    p2 = r1.astype(BF16)
    p3 = (r1 - p2.astype(F32)).astype(BF16)
    return _dot_nt(eye, p1) + _dot_nt(eye, p2) + _dot_nt(eye, p3)


def _permute_exact(p, x):
    p1 = x.astype(BF16)
    r1 = x - p1.astype(F32)
    p2 = r1.astype(BF16)
    p3 = (r1 - p2.astype(F32)).astype(BF16)
    return _dot(p, p1) + _dot(p, p2) + _dot(p, p3)


def _post_kernel(xp_ref, xs_ref, cp_ref, cs_ref, yp_ref, ys_ref, gp_ref, gs_ref, bp_ref, bs_ref,
                 p_ref, rot_ref, eye_ref, avg_ref, gng_ref, gnb_ref, wout_ref, ln_g_ref, ln_b_ref, wt_ref, bias_ref,
                 x1p_ref, x1s_ref, gtp_ref, gts_ref, *, alpha):
    i = pl.program_id(0)
    is_s = i == pl.num_programs(0) - 1
    pick = lambda p, s: jnp.where(is_s, s[...], p[...])
    c, g, bonus = pick(cp_ref, cs_ref), pick(gp_ref, gs_ref), pick(bp_ref, bs_ref)
    y = jnp.where(is_s, ys_ref[...], _permute_exact(rot_ref[...], yp_ref[...]))
    mu = _split_dot(y, avg_ref[...])
    d = y - mu
    var = _split_dot(d * d, avg_ref[...])
    yn = d * lax.rsqrt(var + GN_EPS) * gng_ref[...] + gnb_ref[...]
    o = ((yn + bonus) * g).astype(BF16)
    lhs = jnp.concatenate([c, o], axis=1)
    lhs = jnp.where(is_s, lhs, _dot(p_ref[...], lhs).astype(BF16))
    h = _dot(lhs, wout_ref[...])
    x = jnp.where(is_s, xs_ref[...], xp_ref[...].reshape(ROWS, xp_ref.shape[-1]))
    x1 = _layer_norm(alpha * x + h, ln_g_ref[...], ln_b_ref[...])
    gates = _transpose_exact(_route(x1.astype(BF16), wt_ref[...], bias_ref[...]), eye_ref[...])

    @pl.when(jnp.logical_not(is_s))
    def _():
        x1p_ref[...] = x1.reshape(x1p_ref.shape)
        gtp_ref[...] = gates.reshape(gtp_ref.shape)

    @pl.when(is_s)
    def _():
        x1s_ref[...] = x1
        gts_ref[...] = gates


def _post(xp, xs, cp, cs, yp, ys, gp, gs, bp, bs, consts, alpha):
    nb, t, d = xp.shape
    n_tiles = t // GROUP
    ns = xs.shape[0]
    dr = yp.shape[1]
    n_exp = consts[-2].shape[0]
    pm = lambda w: pl.BlockSpec((ROWS, w), lambda i: (jnp.minimum(i, n_tiles - 1), 0))
    sm = lambda w: pl.BlockSpec((ROWS, w), lambda i: (0, 0))
    tile3 = lambda w: pl.BlockSpec((nb, GROUP, w), lambda i: (0, jnp.minimum(i, n_tiles - 1), 0))
    return pl.pallas_call(
        functools.partial(_post_kernel, alpha=alpha),
        grid=(n_tiles + 1,),
        in_specs=[tile3(d), sm(d), pm(cp.shape[1]), sm(cp.shape[1]), pm(dr), sm(dr), pm(dr), sm(dr), pm(dr), sm(dr)]
                 + [_full(a.shape) for a in consts],
        out_specs=[tile3(d), sm(d), tile3(n_exp), sm(n_exp)],
        out_shape=[jax.ShapeDtypeStruct((nb, t, d), F32), jax.ShapeDtypeStruct((ns, d), F32),
                   jax.ShapeDtypeStruct((nb, t, n_exp), F32), jax.ShapeDtypeStruct((ns, n_exp), F32)],
        compiler_params=_params("arbitrary"),
        name="post",
    )(xp, xs, cp, cs, yp, ys, gp, gs, bp, bs, *consts)


def _experts_kernel(xp_ref, xs_ref, gp_ref, gs_ref, wg_ref, wu_ref, wd_ref, sg_ref, su_ref, sd_ref,
                    ln_g_ref, ln_b_ref, yp_ref, ys_ref, xb_scr, g_scr, *, alpha, n_valid):
    i, e = pl.program_id(0), pl.program_id(1)
    last_i = pl.num_programs(0) - 1
    eps = wg_ref.shape[0]

    @pl.when(e == 0)
    def _():
        @pl.when(i < last_i)
        def _():
            xb_scr[...] = xp_ref[...].astype(BF16)
            g_scr[...] = gp_ref[...]

        @pl.when(i == last_i)
        def _():
            xb_scr[:n_valid, :] = xp_ref[:n_valid, :].astype(BF16)
            xb_scr[n_valid:, :] = xs_ref[...].astype(BF16)
            g_scr[:n_valid, :] = gp_ref[:n_valid, :]
            g_scr[n_valid:, :] = gs_ref[...]

        x0 = xb_scr[...]
        hs = _silu(_dot(x0, sg_ref[...])) * _dot(x0, su_ref[...])
        yp_ref[...] = _dot(hs.astype(BF16), sd_ref[...])

    x = xb_scr[...]
    lane = lax.broadcasted_iota(jnp.int32, g_scr.shape, 1)
    hcat = []
    for q in range(eps):
        h = _silu(_dot(x, wg_ref[q].astype(BF16))) * _dot(x, wu_ref[q].astype(BF16))
        gcol = jnp.sum(jnp.where(lane == e * eps + q, g_scr[...], 0.0), axis=1, keepdims=True)
        hcat.append((h * gcol).astype(BF16))
    wd = wd_ref[...].reshape(eps * wd_ref.shape[1], wd_ref.shape[2]).astype(BF16)
    yp_ref[...] += _dot(jnp.concatenate(hcat, axis=1), wd)

    @pl.when(e == pl.num_programs(1) - 1)
    def _():
        ln = lambda xv, fv: _layer_norm(alpha * xv + fv, ln_g_ref[...], ln_b_ref[...])

        @pl.when(i < last_i)
        def _():
            yp_ref[...] = ln(xp_ref[...], yp_ref[...])

        @pl.when(i == last_i)
        def _():
            ys_ref[...] = ln(xs_ref[...], yp_ref[n_valid:, :])
            yp_ref[:n_valid, :] = ln(xp_ref[:n_valid, :], yp_ref[:n_valid, :])


def _experts(x1p, x1s, gp, gs, wg, wu, wd, sg, su, sd, ln_g, ln_b, alpha):
    n_p, d = x1p.shape
    ns = x1s.shape[0]
    n = n_p + ns
    n_exp, _, de = wg.shape
    eps = 2
    rows = max(r for r in range(16, min(n, MOE_ROWS) + 1, 16) if n % r == 0)
    n_tiles = n // rows
    n_valid = n_p - (n_tiles - 1) * rows
    assert n_exp % eps == 0 and n_valid + ns == rows and n_valid % 16 == 0
    tile = lambda w: pl.BlockSpec((rows, w), lambda i, e: (i, 0))
    whole = lambda a: pl.BlockSpec(a.shape, lambda i, e: (0,) * a.ndim)
    wspec = lambda a: pl.BlockSpec((eps,) + a.shape[1:], lambda i, e: (e, 0, 0))
    return pl.pallas_call(
        functools.partial(_experts_kernel, alpha=alpha, n_valid=n_valid),
        grid=(n_tiles, n_exp // eps),
        in_specs=[tile(d), whole(x1s), tile(n_exp), whole(gs), wspec(wg), wspec(wu), wspec(wd),
                  whole(sg), whole(su), whole(sd), whole(ln_g), whole(ln_b)],
        out_specs=[tile(d), whole(x1s)],
        out_shape=[jax.ShapeDtypeStruct((n_p, d), F32), jax.ShapeDtypeStruct((ns, d), F32)],
        scratch_shapes=[pltpu.VMEM((rows, d), BF16), pltpu.VMEM((rows, n_exp), F32)],
        compiler_params=pltpu.CompilerParams(dimension_semantics=("arbitrary", "arbitrary"),
                                             vmem_limit_bytes=MOE_VMEM_LIMIT),
        name="experts",
    )(x1p, x1s, gp, gs, wg, wu, wd, sg, su, sd, ln_g, ln_b)


def _tile_permutation():
    p = np.zeros((ROWS, ROWS), np.float32)
    for b in range(N_SEQ):
        for t in range(GROUP):
            p[b * GROUP + t, t * N_SEQ + b] = 1.0
    return p


def _row(a):
    return a.reshape(1, -1).astype(F32)


def _layer(xp, xs, conv_s, shift_s, wkv_s, lp, alpha):
    nb, t, d = xp.shape
    ns = xs.shape[0]
    assert nb == N_SEQ and ns == ROWS and t % WKV_TOKENS == 0
    n_p = nb * t
    dc = lp["conv_w"].shape[1]
    dr = lp["w0"].shape[0]
    n_heads = dr // HEAD_DIM
    lw, la, lg = lp["w_up"].shape[0], lp["a_up"].shape[0], lp["g_up"].shape[0]
    lora = lw + la + lg
    lora_pad = -(-lora // LANES) * LANES

    w_in = lp["w_in"]
    wcv = w_in[:, :2 * dc].astype(BF16)
    wrkv = w_in[:, 2 * dc:2 * dc + 3 * dr].astype(BF16)
    wlo = jnp.pad(w_in[:, 2 * dc + 3 * dr:], ((0, 0), (0, lora_pad - lora))).astype(BF16)
    mu = lp["mu_shift"]
    mur = _row(mu[:3 * dr])
    mul = _row(jnp.pad(mu[3 * dr:], (0, lora_pad - lora)))
    wup = jnp.pad(lp["w_up"], ((0, lora_pad - lw), (0, 0))).astype(BF16)
    aup = jnp.pad(lp["a_up"], ((lw, lora_pad - lw - la), (0, 0))).astype(BF16)
    gup = jnp.pad(lp["g_up"], ((lw + la, lora_pad - lora), (0, 0))).astype(BF16)
    head_of = np.arange(dr) // HEAD_DIM
    same_head = (head_of[:, None] == head_of[None, :]).astype(np.float32)
    ones_bd = jnp.asarray(same_head, BF16)
    avg_bd = jnp.asarray(same_head / HEAD_DIM, BF16)
    perm = _tile_permutation()
    prm = [mur, mul, _row(lp["w0"]), _row(lp["a0"]), _row(lp["k_k"]), _row(lp["k_a"]), _row(lp["r_k"]),
           wup, aup, gup, ones_bd]
    conv_prm = [lp["conv_w"].astype(F32), _row(lp["conv_b"]), _row(lp["conv_ln_g"]), _row(lp["conv_ln_b"])]

    u, zr, zl = _in_proj(xp, xs.reshape(ns, d), jnp.asarray(perm.T, BF16), wcv, wrkv, wlo)

    c_p = _conv_seq(u, n_p, *conv_prm)
    c_s = _conv_batch(conv_s.transpose(1, 0, 2), u, n_p, *conv_prm)

    shift_r = shift_s[:, :3 * dr]
    shift_l = jnp.pad(shift_s[:, 3 * dr:], ((0, 0), (0, lora_pad - lora)))
    pre_p = _rwkv_pre(zr, zl, n_p, 0, SEQ_TILE_ROWS, prm)
    pre_s = _rwkv_pre(zr, zl, ns, n_p, ROWS, prm, state=(shift_r, shift_l))

    win = _window_masks()
    hm = jnp.asarray(head_of[None, :] == np.arange(n_heads)[:, None], F32)

    def run_wkv(pre, vc, nseq, tt, tokens, s0, unrotate):
        r, w, k, a, b = [z.reshape(tt, nseq, dr) for z in pre[:5]]
        ycol, s_new = _wkv(r, w, k, a, b, vc, win, hm, s0, tokens)
        return _from_columns(ycol, min(GROUP, tokens), nseq, unrotate).reshape(tt * nseq, dr), s_new

    y_p, st_p = run_wkv(pre_p, pre_p[V_OUT], nb, t, WKV_TOKENS, jnp.zeros((nb * HEAD_DIM, dr), F32), False)
    s0_s = wkv_s.transpose(0, 2, 1, 3).reshape(ns * HEAD_DIM, dr)
    y_s, st_s = run_wkv(pre_s, _to_columns(pre_s[V_OUT].reshape(1, ns, dr), 1), ns, 1, 1, s0_s, True)
    rot = np.zeros((ROWS, ROWS), np.float32)
    for j in range(GROUP):
        for b in range(N_SEQ):
            rot[j * N_SEQ + b, ((j + 2 * b) % GROUP) * N_SEQ + b] = 1.0

    n_exp = lp["router_w"].shape[1]
    bias_b = jnp.broadcast_to(lp["router_bias"].astype(F32)[:, None], (n_exp, ROWS))
    post_consts = [jnp.asarray(perm, BF16), jnp.asarray(rot, BF16), jnp.asarray(np.eye(ROWS), BF16), avg_bd,
                   _row(lp["gn_g"]),
                   _row(lp["gn_b"]), lp["w_out"].astype(BF16), _row(lp["ln1_g"]), _row(lp["ln1_b"]),
                   lp["router_w"].T.astype(BF16), bias_b]
    x1p, x1s, gates_p, gates_s = _post(xp, xs.reshape(ns, d), c_p, c_s, y_p, y_s, pre_p[6], pre_s[6],
                                       pre_p[7], pre_s[7], post_consts, alpha)

    y_prompt, y_sample = _experts(x1p.reshape(n_p, d), x1s, gates_p.reshape(n_p, n_exp), gates_s,
                                  lp["exp_gate"], lp["exp_up"], lp["exp_down"], lp["sh_gate"].astype(BF16),
                                  lp["sh_up"].astype(BF16), lp["sh_down"].astype(BF16),
                                  _row(lp["ln2_g"]), _row(lp["ln2_b"]), alpha)
    y_prompt = y_prompt.reshape(nb, t, d)

    taps = lp["conv_w"].shape[0]
    new_conv_p = u[n_p - (taps - 1) * nb:n_p].reshape(taps - 1, nb, dc).transpose(1, 0, 2)
    new_conv_s = jnp.concatenate([conv_s[:, 1:], u[n_p:, None, :]], axis=1)
    z_rows = lambda lo, hi: jnp.concatenate([zr[lo:hi], zl[lo:hi, :lora]], axis=1)
    new_shift_p = z_rows(n_p - nb, n_p)
    new_shift_s = z_rows(n_p, n_p + ns)
    unstate = lambda s, nseq: s.reshape(nseq, HEAD_DIM, n_heads, HEAD_DIM).transpose(0, 2, 1, 3)
    return (y_prompt, y_sample.reshape(ns, 1, d), new_conv_p, new_shift_p, unstate(st_p, nb),
            new_conv_s, new_shift_s, unstate(st_s, ns))


def kernel(x_prompt, x_sample, state_conv, state_shift, state_wkv, w_in, mu_shift, conv_w, conv_b, conv_ln_g, conv_ln_b, w0, w_up, a0, a_up, g_up, k_k, k_a, r_k, gn_g, gn_b, w_out, ln1_g, ln1_b, router_w, router_bias, exp_gate, exp_up, exp_down, sh_gate, sh_up, sh_down, ln2_g, ln2_b):
    depth = w_in.shape[0]
    alpha = (2.0 * depth) ** 0.25
    names = ("w_in", "mu_shift", "conv_w", "conv_b", "conv_ln_g", "conv_ln_b", "w0", "w_up", "a0", "a_up", "g_up",
             "k_k", "k_a", "r_k", "gn_g", "gn_b", "w_out", "ln1_g", "ln1_b", "router_w", "router_bias", "exp_gate",
             "exp_up", "exp_down", "sh_gate", "sh_up", "sh_down", "ln2_g", "ln2_b")
    vals = (w_in, mu_shift, conv_w, conv_b, conv_ln_g, conv_ln_b, w0, w_up, a0, a_up, g_up, k_k, k_a, r_k, gn_g,
            gn_b, w_out, ln1_g, ln1_b, router_w, router_bias, exp_gate, exp_up, exp_down, sh_gate, sh_up, sh_down,
            ln2_g, ln2_b)
    xp, xs = x_prompt, x_sample
    outs = [[] for _ in range(6)]
    for l in range(depth):
        lp = {n: v[l] for n, v in zip(names, vals)}
        assert xs.shape[1] == 1
        xp, xs, *states = _layer(xp, xs, state_conv[l], state_shift[l], state_wkv[l], lp, alpha)
        for o, s in zip(outs, states):
            o.append(s)
    return (xp, xs) + tuple(jnp.stack(o) for o in outs)
```

```python
import functools

import jax
import jax.numpy as jnp
import numpy as np
from jax import lax
from jax.experimental import pallas as pl
from jax.experimental.pallas import tpu as pltpu

F32 = jnp.float32
BF16 = jnp.bfloat16

HEAD_DIM = 64
N_SEQ = 8
GROUP = 16
ROWS = GROUP * N_SEQ
LANES = 128
LN_EPS = 1e-5
GN_EPS = 64e-5
ROUTED_SCALE = 2.5
N_GROUPS = 8
TOPK_GROUPS = 4
TOP_K = 8
SEQ_TILE_ROWS = 512
WKV_TOKENS = 64
WKV_CHAINS = 2
MOE_ROWS = 1376
VMEM_LIMIT = 48 * 1024 * 1024
MOE_VMEM_LIMIT = 56 * 1024 * 1024


def _params(*sem):
    return pltpu.CompilerParams(dimension_semantics=sem, vmem_limit_bytes=VMEM_LIMIT)


def _full(shape):
    n = len(shape)
    return pl.BlockSpec(shape, lambda *_: (0,) * n)


def _dot(a, b):
    return jnp.dot(a, b, preferred_element_type=F32)


def _dot_nt(a, b):
    return lax.dot_general(a, b, (((1,), (1,)), ((), ())), preferred_element_type=F32)


def _split_dot(x, m):
    hi = x.astype(BF16)
    lo = (x - hi.astype(F32)).astype(BF16)
    return _dot(hi, m) + _dot(lo, m)


def _layer_norm(x, g, b):
    mu = jnp.mean(x, axis=-1, keepdims=True)
    d = x - mu
    var = jnp.mean(d * d, axis=-1, keepdims=True)
    return d * lax.rsqrt(var + LN_EPS) * g + b


def _silu(x):
    return x * jax.nn.sigmoid(x)


def _in_proj_kernel(xp_ref, xs_ref, pt_ref, wcv_ref, wrkv_ref, wlo_ref, u_ref, zr_ref, zl_ref):
    i = pl.program_id(0)
    last = pl.num_programs(0) - 1
    xb = xp_ref[...].reshape(ROWS, xp_ref.shape[-1]).astype(BF16)
    xt = _dot(pt_ref[...], xb).astype(BF16)
    x = jnp.where(i == last, xs_ref[...].astype(BF16), xt)
    zc = _dot(x, wcv_ref[...])
    dc = zc.shape[1] // 2
    u_ref[...] = zc[:, :dc] * jax.nn.sigmoid(zc[:, dc:])
    zr_ref[...] = _dot(x, wrkv_ref[...])
    zl_ref[...] = _dot(x, wlo_ref[...])


def _in_proj(xp, xs, pt, wcv, wrkv, wlo):
    nb, t, d = xp.shape
    n_tiles = t // GROUP
    n = n_tiles * ROWS + xs.shape[0]
    dc, dr, dl = wcv.shape[1] // 2, wrkv.shape[1], wlo.shape[1]
    row = lambda w: pl.BlockSpec((ROWS, w), lambda i: (i, 0))
    return pl.pallas_call(
        _in_proj_kernel,
        grid=(n_tiles + 1,),
        in_specs=[pl.BlockSpec((nb, GROUP, d), lambda i: (0, jnp.minimum(i, n_tiles - 1), 0)),
                  _full(xs.shape), _full(pt.shape), _full(wcv.shape), _full(wrkv.shape), _full(wlo.shape)],
        out_specs=[row(dc), row(dr), row(dl)],
        out_shape=[jax.ShapeDtypeStruct((n, dc), F32), jax.ShapeDtypeStruct((n, dr), F32),
                   jax.ShapeDtypeStruct((n, dl), F32)],
        compiler_params=_params("parallel"),
        name="in_proj",
    )(xp, xs, pt, wcv, wrkv, wlo)


def _conv_tail(acc, g, b):
    return _silu(_layer_norm(acc, g, b)).astype(BF16)


def _conv_seq_kernel(u_ref, w_ref, b_ref, g_ref, beta_ref, c_ref, hist_ref, *, taps, chunk):
    rows = u_ref.shape[0]
    hr = (taps - 1) * N_SEQ

    @pl.when(pl.program_id(0) == 0)
    def _():
        hist_ref[0:hr, :] = jnp.zeros((hr, hist_ref.shape[1]), F32)

    hist_ref[hr:hr + rows, :] = u_ref[...]

    def body(ci, carry):
        r0 = pl.multiple_of(ci * chunk, chunk)
        dc = b_ref.shape[1]
        acc = jnp.broadcast_to(b_ref[...], (chunk, dc)).reshape(chunk // N_SEQ, N_SEQ, dc)
        for j in range(taps):
            win = hist_ref[pl.ds(pl.multiple_of(r0 + j * N_SEQ, N_SEQ), chunk), :]
            acc = acc + w_ref[j] * win.reshape(chunk // N_SEQ, N_SEQ, dc)
        c_ref[pl.ds(r0, chunk), :] = _conv_tail(acc.reshape(chunk, dc), g_ref[...], beta_ref[...])
        return carry

    lax.fori_loop(0, rows // chunk, body, 0)
    hist_ref[0:hr, :] = hist_ref[rows:rows + hr, :]


def _conv_seq(u, n_rows, w, b, g, beta):
    taps, dc = w.shape
    rows = SEQ_TILE_ROWS
    hr = (taps - 1) * N_SEQ
    assert rows >= hr and n_rows % rows == 0
    w = jnp.broadcast_to(w[:, None, :], (taps, N_SEQ, dc))
    return pl.pallas_call(
        functools.partial(_conv_seq_kernel, taps=taps, chunk=64),
        grid=(n_rows // rows,),
        in_specs=[pl.BlockSpec((rows, dc), lambda i: (i, 0)), _full(w.shape), _full(b.shape),
                  _full(g.shape), _full(beta.shape)],
        out_specs=pl.BlockSpec((rows, dc), lambda i: (i, 0)),
        out_shape=jax.ShapeDtypeStruct((n_rows, dc), BF16),
        scratch_shapes=[pltpu.VMEM((hr + rows, dc), F32)],
        compiler_params=_params("arbitrary"),
        name="conv_seq",
    )(u, w, b, g, beta)


def _conv_batch_kernel(cs_ref, u_ref, w_ref, b_ref, g_ref, beta_ref, c_ref, *, taps):
    acc = b_ref[...] + w_ref[taps - 1:taps, :] * u_ref[...]
    for j in range(taps - 1):
        acc = acc + w_ref[j:j + 1, :] * cs_ref[j]
    c_ref[...] = _conv_tail(acc, g_ref[...], beta_ref[...])


def _conv_batch(conv_t, u, row0, w, b, g, beta):
    taps, dc = w.shape
    ns = conv_t.shape[1]
    bs = 32
    assert ns % bs == 0 and row0 % bs == 0
    return pl.pallas_call(
        functools.partial(_conv_batch_kernel, taps=taps),
        grid=(ns // bs,),
        in_specs=[pl.BlockSpec((taps - 1, bs, dc), lambda i: (0, i, 0)),
                  pl.BlockSpec((bs, dc), lambda i: (row0 // bs + i, 0)),
                  _full(w.shape), _full(b.shape), _full(g.shape), _full(beta.shape)],
        out_specs=pl.BlockSpec((bs, dc), lambda i: (i, 0)),
        out_shape=jax.ShapeDtypeStruct((ns, dc), BF16),
        compiler_params=_params("parallel"),
        name="conv_batch",
    )(conv_t, u, w, b, g, beta)


def _rwkv_pre_math(zr, zl, zr_prev, zl_prev, mur, mul, w0, a0, kk_w, ka_w, rk_w, wup, aup, gup, ones_bd):
    dr = w0.shape[1]
    zsr = zr + (zr_prev - zr) * mur
    zsl = zl + (zl_prev - zl) * mul
    r, k, v = zsr[:, :dr], zsr[:, dr:2 * dr], zsr[:, 2 * dr:]
    lw = w0 + _dot(jnp.tanh(zsl).astype(BF16), wup)
    w_log = -(jnp.maximum(-lw, 0.0) + jnp.log(1.0 + jnp.exp(-jnp.abs(lw)))) - 0.5
    decay = jnp.exp(-jnp.exp(w_log))
    a = jax.nn.sigmoid(a0 + _dot(zsl.astype(BF16), aup))
    g = _dot(jax.nn.sigmoid(zsl).astype(BF16), gup)
    kk = k * kk_w
    ss = _split_dot(kk * kk, ones_bd)
    kk = kk / jnp.maximum(jnp.sqrt(ss), 1e-12)
    k2 = k * (1.0 + (a - 1.0) * ka_w)
    bonus = _split_dot(r * k2 * rk_w, ones_bd) * v
    return r, decay, k2, -kk, kk * a, v, g, bonus


N_PRE_PARAMS = 11
N_PRE_OUTS = 8


V_OUT = 5


def _rwkv_pre_seq_kernel(zr_ref, zl_ref, *refs):
    prm, q_ref = refs[:N_PRE_PARAMS], refs[N_PRE_PARAMS]
    outs = refs[N_PRE_PARAMS + 1:N_PRE_PARAMS + 1 + N_PRE_OUTS]
    car_r, car_l = refs[N_PRE_PARAMS + 1 + N_PRE_OUTS:]

    @pl.when(pl.program_id(0) == 0)
    def _():
        car_r[...] = jnp.zeros(car_r.shape, F32)
        car_l[...] = jnp.zeros(car_l.shape, F32)

    zr, zl = zr_ref[...], zl_ref[...]
    zr_prev = jnp.concatenate([car_r[...], zr[:-N_SEQ]], axis=0)
    zl_prev = jnp.concatenate([car_l[...], zl[:-N_SEQ]], axis=0)
    res = _rwkv_pre_math(zr, zl, zr_prev, zl_prev, *[p[...] for p in prm])
    for n, (o, val) in enumerate(zip(outs, res)):
        if n != V_OUT:
            o[...] = val
    v, vc_ref = res[V_OUT], outs[V_OUT]
    n_heads = v.shape[1] // HEAD_DIM
    n_groups = v.shape[0] // ROWS
    tcats = []
    for gi in range(n_groups):
        t1 = v[gi * ROWS:(gi + 1) * ROWS, :].T
        tcat = jnp.concatenate([t1[h * HEAD_DIM:(h + 1) * HEAD_DIM, :] for h in range(n_heads)], axis=1)
        tcats.append(tcat.astype(BF16))
    tcat_all = jnp.concatenate(tcats, axis=0)
    for b in range(N_SEQ):
        cols = _dot(tcat_all, q_ref[b]).astype(BF16)
        for gi in range(n_groups):
            vc_ref[gi, b * HEAD_DIM:(b + 1) * HEAD_DIM, :] = cols[gi * HEAD_DIM:(gi + 1) * HEAD_DIM, :]
    car_r[...] = zr[-N_SEQ:]
    car_l[...] = zl[-N_SEQ:]


def _rwkv_pre_batch_kernel(zr_ref, zl_ref, sr_ref, sl_ref, *refs):
    prm, outs = refs[:N_PRE_PARAMS], refs[N_PRE_PARAMS:]
    res = _rwkv_pre_math(zr_ref[...], zl_ref[...], sr_ref[...], sl_ref[...], *[p[...] for p in prm])
    for o, val in zip(outs, res):
        o[...] = val


def _column_selectors(n_heads):
    q = np.zeros((N_SEQ, n_heads * ROWS, 2 * LANES), np.float32)
    for b in range(N_SEQ):
        for h in range(n_heads):
            for j in range(GROUP):
                q[b, h * ROWS + j * N_SEQ + b, (j % 2) * LANES + (16 * b + 8 * j + h) % LANES] = 1.0
    return q


def _rwkv_pre(zr, zl, n_rows, row0, rows, prm, state=None):
    dr3, dl = zr.shape[1], zl.shape[1]
    dr = dr3 // 3
    assert n_rows % rows == 0 and row0 % rows == 0 and rows % ROWS == 0
    zspec = lambda w: pl.BlockSpec((rows, w), lambda i: (row0 // rows + i, 0))
    ospec = pl.BlockSpec((rows, dr), lambda i: (i, 0))
    common = dict(
        grid=(n_rows // rows,),
        out_specs=[ospec] * N_PRE_OUTS,
        out_shape=[jax.ShapeDtypeStruct((n_rows, dr), F32)] * N_PRE_OUTS,
    )
    assert len(prm) == N_PRE_PARAMS
    pspecs = [_full(p.shape) for p in prm]
    if state is None:
        q = jnp.asarray(_column_selectors(dr // HEAD_DIM), BF16)
        srows = N_SEQ * HEAD_DIM
        common["out_specs"][V_OUT] = pl.BlockSpec((rows // ROWS, srows, 2 * LANES), lambda i: (i, 0, 0))
        common["out_shape"][V_OUT] = jax.ShapeDtypeStruct((n_rows // ROWS, srows, 2 * LANES), BF16)
        return pl.pallas_call(
            _rwkv_pre_seq_kernel,
            in_specs=[zspec(dr3), zspec(dl)] + pspecs + [_full(q.shape)],
            scratch_shapes=[pltpu.VMEM((N_SEQ, dr3), F32), pltpu.VMEM((N_SEQ, dl), F32)],
            compiler_params=_params("arbitrary"),
            name="rwkv_pre_seq",
            **common,
        )(zr, zl, *prm, q)
    sr, sl = state
    sspec = lambda w: pl.BlockSpec((rows, w), lambda i: (i, 0))
    return pl.pallas_call(
        _rwkv_pre_batch_kernel,
        in_specs=[zspec(dr3), zspec(dl), sspec(dr3), sspec(dl)] + pspecs,
        compiler_params=_params("parallel"),
        name="rwkv_pre_batch",
        **common,
    )(zr, zl, sr, sl, *prm)


def _wkv_kernel(r_ref, w_ref, k_ref, a_ref, b_ref, vc_ref, win_ref, hm_ref, s0_ref,
                y_ref, st_ref, s_scr, *, n_tok):
    n_groups = vc_ref.shape[0]
    hm = hm_ref[...]
    zeros8 = jnp.zeros(hm.shape, F32)

    @pl.when(pl.program_id(1) == 0)
    def _():
        s_scr[...] = s0_ref[...]

    def masked_rows(x8, b):
        return x8[b:b + 1, :] * hm

    def group(gi, carry):
        seq_per_chain = N_SEQ // WKV_CHAINS
        chain_rows = seq_per_chain * HEAD_DIM
        yacc = [jnp.zeros((chain_rows, LANES), F32)] * WKV_CHAINS
        for j in range(n_tok + 1):
            t = gi * n_tok + j
            blocks = [zeros8] * 32
            if j < n_tok:
                a8 = a_ref[t]
                for b in range(N_SEQ):
                    blocks[((16 * b + 8 * j + 8) % LANES) // 8] = masked_rows(a8, b)
            if j > 0:
                r8 = r_ref[t - 1]
                for b in range(N_SEQ):
                    blocks[16 + ((16 * b + 8 * (j - 1)) % LANES) // 8] = masked_rows(r8, b)
            rt = jnp.concatenate(blocks, axis=0).astype(BF16)
            if j < n_tok:
                b8, k8, w8 = b_ref[t], k_ref[t], w_ref[t]
                blocks2 = [zeros8] * 16
                for b in range(N_SEQ):
                    blocks2[((16 * b + 8 * j) % LANES) // 8] = masked_rows(k8, b)
                    blocks2[((16 * b + 8 * j + 8) % LANES) // 8] = masked_rows(b8, b)
                rhs2 = jnp.concatenate(blocks2, axis=0).astype(BF16)
                half = slice(0, LANES) if j % 2 == 0 else slice(LANES, 2 * LANES)
            for ch in range(WKV_CHAINS):
                rows = slice(ch * chain_rows, (ch + 1) * chain_rows)
                c = _dot_nt(s_scr[rows, :].astype(BF16), rt)
                if j > 0:
                    yacc[ch] = yacc[ch] + c[:, LANES:] * win_ref[j - 1, rows, :]
                if j < n_tok:
                    vcol = vc_ref[gi, rows, half].astype(F32)
                    lhs2 = ((c[:, :LANES] + vcol) * win_ref[j, rows, :]).astype(BF16)
                    d = _dot(lhs2, rhs2)
                    for b in range(ch * seq_per_chain, (ch + 1) * seq_per_chain):
                        sl = slice(b * HEAD_DIM, (b + 1) * HEAD_DIM)
                        dl = slice(sl.start - rows.start, sl.stop - rows.start)
                        s_scr[sl, :] = s_scr[sl, :] * w8[b:b + 1, :] + d[dl, :]
        for ch in range(WKV_CHAINS):
            y_ref[gi, ch * chain_rows:(ch + 1) * chain_rows, :] = yacc[ch]
        return carry

    lax.fori_loop(0, n_groups, group, 0)

    @pl.when(pl.program_id(1) == pl.num_programs(1) - 1)
    def _():
        st_ref[...] = s_scr[...]


def _wkv(r, w, k, a, b, vc, win, hm, s0, tokens):
    t, nseq, dr = r.shape
    n_tok = min(GROUP, tokens)
    assert t % tokens == 0 and tokens % n_tok == 0 and nseq % N_SEQ == 0
    assert dr // HEAD_DIM == LANES // GROUP
    gps = tokens // n_tok
    srows = N_SEQ * HEAD_DIM
    tspec = pl.BlockSpec((tokens, N_SEQ, dr), lambda bb, c: (c, bb, 0))
    vspec = lambda w_: pl.BlockSpec((gps, srows, w_), lambda bb, c: (c, bb, 0))
    sspec = pl.BlockSpec((srows, dr), lambda bb, c: (bb, 0))
    return pl.pallas_call(
        functools.partial(_wkv_kernel, n_tok=n_tok),
        grid=(nseq // N_SEQ, t // tokens),
        in_specs=[tspec] * 5 + [vspec(2 * LANES), _full(win.shape), _full(hm.shape), sspec],
        out_specs=[vspec(LANES), sspec],
        out_shape=[jax.ShapeDtypeStruct(vc.shape[:2] + (LANES,), F32), jax.ShapeDtypeStruct(s0.shape, F32)],
        scratch_shapes=[pltpu.VMEM((srows, dr), F32)],
        compiler_params=_params("parallel", "arbitrary"),
        name="wkv",
    )(r, w, k, a, b, vc, win, hm, s0)


def _window_masks():
    lane = np.arange(LANES)[None, None, :]
    b = (np.arange(N_SEQ * HEAD_DIM) // HEAD_DIM)[None, :, None]
    j = np.arange(GROUP)[:, None, None]
    return jnp.asarray(((lane - 16 * b - 8 * j) % LANES) < 16, F32)


def _to_columns(v, n_tok):
    t, nseq, dr = v.shape
    h = dr // HEAD_DIM
    g = t // n_tok
    x = v.reshape(g, n_tok, nseq // N_SEQ, N_SEQ, h, HEAD_DIM)
    x = jnp.pad(x, ((0, 0), (0, GROUP - n_tok)) + ((0, 0),) * 4)
    x = x.transpose(0, 2, 3, 5, 1, 4).reshape(g, nseq // N_SEQ, N_SEQ, HEAD_DIM, LANES)
    x = jnp.stack([jnp.roll(x[:, :, b], 16 * b, axis=-1) for b in range(N_SEQ)], axis=2)
    x = x.reshape(g, nseq * HEAD_DIM, LANES)
    even = ((jnp.arange(LANES) // 8) % 2 == 0)
    return jnp.concatenate([jnp.where(even, x, 0.0), jnp.where(even, 0.0, x)], axis=-1).astype(BF16)


def _from_columns(y, n_tok, nseq, unrotate=True):
    g = y.shape[0]
    x = y.reshape(g, nseq // N_SEQ, N_SEQ, HEAD_DIM, LANES)
    if unrotate:
        x = jnp.stack([jnp.roll(x[:, :, b], -16 * b, axis=-1) for b in range(N_SEQ)], axis=2)
    x = x.reshape(g, nseq // N_SEQ, N_SEQ, HEAD_DIM, GROUP, LANES // GROUP)[:, :, :, :, :n_tok]
    x = x.transpose(0, 4, 1, 2, 5, 3)
    return x.reshape(g * n_tok, nseq, (LANES // GROUP) * HEAD_DIM)


def _reduce01(fn, x):
    return fn(fn(x, axis=1, keepdims=True), axis=0, keepdims=True)


def _route(x, wt, bias):
    n_exp = wt.shape[0]
    per = n_exp // N_GROUPS
    logits = _dot_nt(wt, x)
    scores = jax.nn.sigmoid(logits)
    sel = (scores + bias).reshape(N_GROUPS, per, ROWS)
    sc3 = scores.reshape(N_GROUPS, per, ROWS)
    neg = -jnp.inf
    pidx = lax.broadcasted_iota(jnp.int32, sel.shape, 1)
    m1 = jnp.max(sel, axis=1, keepdims=True)
    i1 = jnp.min(jnp.where(sel == m1, pidx, per), axis=1, keepdims=True)
    m2 = jnp.max(jnp.where(pidx == i1, neg, sel), axis=1, keepdims=True)
    gscore = m1 + m2
    gidx = lax.broadcasted_iota(jnp.int32, gscore.shape, 0)
    gkeep = jnp.zeros(gscore.shape, F32)
    for _ in range(TOPK_GROUPS):
        gm = jnp.max(gscore, axis=0, keepdims=True)
        gi = jnp.min(jnp.where(gscore == gm, gidx, N_GROUPS), axis=0, keepdims=True)
        hit = gidx == gi
        gkeep = jnp.where(hit, 1.0, gkeep)
        gscore = jnp.where(hit, neg, gscore)
    cand = jnp.where(gkeep > 0.0, sel, neg)
    eidx = lax.broadcasted_iota(jnp.int32, sel.shape, 0) * per + pidx
    gates = jnp.zeros(sel.shape, F32)
    for _ in range(TOP_K):
        em = _reduce01(jnp.max, cand)
        ei = _reduce01(jnp.min, jnp.where(cand == em, eidx, n_exp))
        hit = eidx == ei
        gates = jnp.where(hit, sc3, gates)
        cand = jnp.where(hit, neg, cand)
    total = _reduce01(jnp.sum, gates)
    gates = gates / (total + 1e-20) * ROUTED_SCALE
    return gates.reshape(n_exp, ROWS)


def _transpose_exact(gt, eye):
    p1 = gt.astype(BF16)
    r1 = gt - p1.astype(F32)
    p2 = r1.astype(BF16)
    p3 = (r1 - p2.astype(F32)).astype(BF16)
    return _dot_nt(eye, p1) + _dot_nt(eye, p2) + _dot_nt(eye, p3)


def _permute_exact(p, x):
    p1 = x.astype(BF16)
    r1 = x - p1.astype(F32)
    p2 = r1.astype(BF16)
    p3 = (r1 - p2.astype(F32)).astype(BF16)
    return _dot(p, p1) + _dot(p, p2) + _dot(p, p3)


def _post_kernel(xp_ref, xs_ref, cp_ref, cs_ref, yp_ref, ys_ref, gp_ref, gs_ref, bp_ref, bs_ref,
                 p_ref, rot_ref, eye_ref, avg_ref, gng_ref, gnb_ref, wout_ref, ln_g_ref, ln_b_ref, wt_ref, bias_ref,
                 x1p_ref, x1s_ref, gtp_ref, gts_ref, *, alpha):
    i = pl.program_id(0)
    is_s = i == pl.num_programs(0) - 1
    pick = lambda p, s: jnp.where(is_s, s[...], p[...])
    c, g, bonus = pick(cp_ref, cs_ref), pick(gp_ref, gs_ref), pick(bp_ref, bs_ref)
    y = jnp.where(is_s, ys_ref[...], _permute_exact(rot_ref[...], yp_ref[...]))
    mu = _split_dot(y, avg_ref[...])
    d = y - mu
    var = _split_dot(d * d, avg_ref[...])
    yn = d * lax.rsqrt(var + GN_EPS) * gng_ref[...] + gnb_ref[...]
    o = ((yn + bonus) * g).astype(BF16)
    lhs = jnp.concatenate([c, o], axis=1)
    lhs = jnp.where(is_s, lhs, _dot(p_ref[...], lhs).astype(BF16))
    h = _dot(lhs, wout_ref[...])
    x = jnp.where(is_s, xs_ref[...], xp_ref[...].reshape(ROWS, xp_ref.shape[-1]))
    x1 = _layer_norm(alpha * x + h, ln_g_ref[...], ln_b_ref[...])
    gates = _transpose_exact(_route(x1.astype(BF16), wt_ref[...], bias_ref[...]), eye_ref[...])

    @pl.when(jnp.logical_not(is_s))
    def _():
        x1p_ref[...] = x1.reshape(x1p_ref.shape)
        gtp_ref[...] = gates.reshape(gtp_ref.shape)

    @pl.when(is_s)
    def _():
        x1s_ref[...] = x1
        gts_ref[...] = gates


def _post(xp, xs, cp, cs, yp, ys, gp, gs, bp, bs, consts, alpha):
    nb, t, d = xp.shape
    n_tiles = t // GROUP
    ns = xs.shape[0]
    dr = yp.shape[1]
    n_exp = consts[-2].shape[0]
    pm = lambda w: pl.BlockSpec((ROWS, w), lambda i: (jnp.minimum(i, n_tiles - 1), 0))
    sm = lambda w: pl.BlockSpec((ROWS, w), lambda i: (0, 0))
    tile3 = lambda w: pl.BlockSpec((nb, GROUP, w), lambda i: (0, jnp.minimum(i, n_tiles - 1), 0))
    return pl.pallas_call(
        functools.partial(_post_kernel, alpha=alpha),
        grid=(n_tiles + 1,),
        in_specs=[tile3(d), sm(d), pm(cp.shape[1]), sm(cp.shape[1]), pm(dr), sm(dr), pm(dr), sm(dr), pm(dr), sm(dr)]
                 + [_full(a.shape) for a in consts],
        out_specs=[tile3(d), sm(d), tile3(n_exp), sm(n_exp)],
        out_shape=[jax.ShapeDtypeStruct((nb, t, d), F32), jax.ShapeDtypeStruct((ns, d), F32),
                   jax.ShapeDtypeStruct((nb, t, n_exp), F32), jax.ShapeDtypeStruct((ns, n_exp), F32)],
        compiler_params=_params("arbitrary"),
        name="post",
    )(xp, xs, cp, cs, yp, ys, gp, gs, bp, bs, *consts)


def _experts_kernel(xp_ref, xs_ref, gp_ref, gs_ref, wg_ref, wu_ref, wd_ref, sg_ref, su_ref, sd_ref,
                    ln_g_ref, ln_b_ref, yp_ref, ys_ref, xb_scr, g_scr, *, alpha, n_valid):
    i, e = pl.program_id(0), pl.program_id(1)
    last_i = pl.num_programs(0) - 1
    eps = wg_ref.shape[0]

    @pl.when(e == 0)
    def _():
        @pl.when(i < last_i)
        def _():
            xb_scr[...] = xp_ref[...].astype(BF16)
            g_scr[...] = gp_ref[...]

        @pl.when(i == last_i)
        def _():
            xb_scr[:n_valid, :] = xp_ref[:n_valid, :].astype(BF16)
            xb_scr[n_valid:, :] = xs_ref[...].astype(BF16)
            g_scr[:n_valid, :] = gp_ref[:n_valid, :]
            g_scr[n_valid:, :] = gs_ref[...]

        x0 = xb_scr[...]
        hs = _silu(_dot(x0, sg_ref[...])) * _dot(x0, su_ref[...])
        yp_ref[...] = _dot(hs.astype(BF16), sd_ref[...])

    x = xb_scr[...]
    lane = lax.broadcasted_iota(jnp.int32, g_scr.shape, 1)
    hcat = []
    for q in range(eps):
        h = _silu(_dot(x, wg_ref[q])) * _dot(x, wu_ref[q])
        gcol = jnp.sum(jnp.where(lane == e * eps + q, g_scr[...], 0.0), axis=1, keepdims=True)
        hcat.append((h * gcol).astype(BF16))
    wd = wd_ref[...].reshape(eps * wd_ref.shape[1], wd_ref.shape[2])
    yp_ref[...] += _dot(jnp.concatenate(hcat, axis=1), wd)

    @pl.when(e == pl.num_programs(1) - 1)
    def _():
        ln = lambda xv, fv: _layer_norm(alpha * xv + fv, ln_g_ref[...], ln_b_ref[...])

        @pl.when(i < last_i)
        def _():
            yp_ref[...] = ln(xp_ref[...], yp_ref[...])

        @pl.when(i == last_i)
        def _():
            ys_ref[...] = ln(xs_ref[...], yp_ref[n_valid:, :])
            yp_ref[:n_valid, :] = ln(xp_ref[:n_valid, :], yp_ref[:n_valid, :])


def _experts(x1p, x1s, gp, gs, wg, wu, wd, sg, su, sd, ln_g, ln_b, alpha):
    n_p, d = x1p.shape
    ns = x1s.shape[0]
    n = n_p + ns
    n_exp, _, de = wg.shape
    eps = 4
    assert wg.dtype == BF16 and wu.dtype == BF16 and wd.dtype == BF16
    rows = max(r for r in range(16, min(n, MOE_ROWS) + 1, 16) if n % r == 0)
    n_tiles = n // rows
    n_valid = n_p - (n_tiles - 1) * rows
    assert n_exp % eps == 0 and n_valid + ns == rows and n_valid % 16 == 0
    tile = lambda w: pl.BlockSpec((rows, w), lambda i, e: (i, 0))
    whole = lambda a: pl.BlockSpec(a.shape, lambda i, e: (0,) * a.ndim)
    wspec = lambda a: pl.BlockSpec((eps,) + a.shape[1:], lambda i, e: (e, 0, 0))
    return pl.pallas_call(
        functools.partial(_experts_kernel, alpha=alpha, n_valid=n_valid),
        grid=(n_tiles, n_exp // eps),
        in_specs=[tile(d), whole(x1s), tile(n_exp), whole(gs), wspec(wg), wspec(wu), wspec(wd),
                  whole(sg), whole(su), whole(sd), whole(ln_g), whole(ln_b)],
        out_specs=[tile(d), whole(x1s)],
        out_shape=[jax.ShapeDtypeStruct((n_p, d), F32), jax.ShapeDtypeStruct((ns, d), F32)],
        scratch_shapes=[pltpu.VMEM((rows, d), BF16), pltpu.VMEM((rows, n_exp), F32)],
        compiler_params=pltpu.CompilerParams(dimension_semantics=("arbitrary", "arbitrary"),
                                             vmem_limit_bytes=MOE_VMEM_LIMIT),
        name="experts",
    )(x1p, x1s, gp, gs, wg, wu, wd, sg, su, sd, ln_g, ln_b)


def _tile_permutation():
    p = np.zeros((ROWS, ROWS), np.float32)
    for b in range(N_SEQ):
        for t in range(GROUP):
            p[b * GROUP + t, t * N_SEQ + b] = 1.0
    return p


def _row(a):
    return a.reshape(1, -1).astype(F32)


def _layer(xp, xs, conv_s, shift_s, wkv_s, lp, alpha):
    nb, t, d = xp.shape
    ns = xs.shape[0]
    assert nb == N_SEQ and ns == ROWS and t % WKV_TOKENS == 0
    n_p = nb * t
    dc = lp["conv_w"].shape[1]
    dr = lp["w0"].shape[0]
    n_heads = dr // HEAD_DIM
    lw, la, lg = lp["w_up"].shape[0], lp["a_up"].shape[0], lp["g_up"].shape[0]
    lora = lw + la + lg
    lora_pad = -(-lora // LANES) * LANES

    w_in = lp["w_in"]
    wcv = w_in[:, :2 * dc].astype(BF16)
    wrkv = w_in[:, 2 * dc:2 * dc + 3 * dr].astype(BF16)
    wlo = jnp.pad(w_in[:, 2 * dc + 3 * dr:], ((0, 0), (0, lora_pad - lora))).astype(BF16)
    mu = lp["mu_shift"]
    mur = _row(mu[:3 * dr])
    mul = _row(jnp.pad(mu[3 * dr:], (0, lora_pad - lora)))
    wup = jnp.pad(lp["w_up"], ((0, lora_pad - lw), (0, 0))).astype(BF16)
    aup = jnp.pad(lp["a_up"], ((lw, lora_pad - lw - la), (0, 0))).astype(BF16)
    gup = jnp.pad(lp["g_up"], ((lw + la, lora_pad - lora), (0, 0))).astype(BF16)
    head_of = np.arange(dr) // HEAD_DIM
    same_head = (head_of[:, None] == head_of[None, :]).astype(np.float32)
    ones_bd = jnp.asarray(same_head, BF16)
    avg_bd = jnp.asarray(same_head / HEAD_DIM, BF16)
    perm = _tile_permutation()
    prm = [mur, mul, _row(lp["w0"]), _row(lp["a0"]), _row(lp["k_k"]), _row(lp["k_a"]), _row(lp["r_k"]),
           wup, aup, gup, ones_bd]
    conv_prm = [lp["conv_w"].astype(F32), _row(lp["conv_b"]), _row(lp["conv_ln_g"]), _row(lp["conv_ln_b"])]

    u, zr, zl = _in_proj(xp, xs.reshape(ns, d), jnp.asarray(perm.T, BF16), wcv, wrkv, wlo)

    c_p = _conv_seq(u, n_p, *conv_prm)
    c_s = _conv_batch(conv_s.transpose(1, 0, 2), u, n_p, *conv_prm)

    shift_r = shift_s[:, :3 * dr]
    shift_l = jnp.pad(shift_s[:, 3 * dr:], ((0, 0), (0, lora_pad - lora)))
    pre_p = _rwkv_pre(zr, zl, n_p, 0, SEQ_TILE_ROWS, prm)
    pre_s = _rwkv_pre(zr, zl, ns, n_p, ROWS, prm, state=(shift_r, shift_l))

    win = _window_masks()
    hm = jnp.asarray(head_of[None, :] == np.arange(n_heads)[:, None], F32)

    def run_wkv(pre, vc, nseq, tt, tokens, s0, unrotate):
        r, w, k, a, b = [z.reshape(tt, nseq, dr) for z in pre[:5]]
        ycol, s_new = _wkv(r, w, k, a, b, vc, win, hm, s0, tokens)
        return _from_columns(ycol, min(GROUP, tokens), nseq, unrotate).reshape(tt * nseq, dr), s_new

    y_p, st_p = run_wkv(pre_p, pre_p[V_OUT], nb, t, WKV_TOKENS, jnp.zeros((nb * HEAD_DIM, dr), F32), False)
    s0_s = wkv_s.transpose(0, 2, 1, 3).reshape(ns * HEAD_DIM, dr)
    y_s, st_s = run_wkv(pre_s, _to_columns(pre_s[V_OUT].reshape(1, ns, dr), 1), ns, 1, 1, s0_s, True)
    rot = np.zeros((ROWS, ROWS), np.float32)
    for j in range(GROUP):
        for b in range(N_SEQ):
            rot[j * N_SEQ + b, ((j + 2 * b) % GROUP) * N_SEQ + b] = 1.0

    n_exp = lp["router_w"].shape[1]
    bias_b = jnp.broadcast_to(lp["router_bias"].astype(F32)[:, None], (n_exp, ROWS))
    post_consts = [jnp.asarray(perm, BF16), jnp.asarray(rot, BF16), jnp.asarray(np.eye(ROWS), BF16), avg_bd,
                   _row(lp["gn_g"]),
                   _row(lp["gn_b"]), lp["w_out"].astype(BF16), _row(lp["ln1_g"]), _row(lp["ln1_b"]),
                   lp["router_w"].T.astype(BF16), bias_b]
    x1p, x1s, gates_p, gates_s = _post(xp, xs.reshape(ns, d), c_p, c_s, y_p, y_s, pre_p[6], pre_s[6],
                                       pre_p[7], pre_s[7], post_consts, alpha)

    y_prompt, y_sample = _experts(x1p.reshape(n_p, d), x1s, gates_p.reshape(n_p, n_exp), gates_s,
                                  lp["exp_gate"].astype(BF16), lp["exp_up"].astype(BF16),
                                  lp["exp_down"].astype(BF16), lp["sh_gate"].astype(BF16),
                                  lp["sh_up"].astype(BF16), lp["sh_down"].astype(BF16),
                                  _row(lp["ln2_g"]), _row(lp["ln2_b"]), alpha)
    y_prompt = y_prompt.reshape(nb, t, d)

    taps = lp["conv_w"].shape[0]
    new_conv_p = u[n_p - (taps - 1) * nb:n_p].reshape(taps - 1, nb, dc).transpose(1, 0, 2)
    new_conv_s = jnp.concatenate([conv_s[:, 1:], u[n_p:, None, :]], axis=1)
    z_rows = lambda lo, hi: jnp.concatenate([zr[lo:hi], zl[lo:hi, :lora]], axis=1)
    new_shift_p = z_rows(n_p - nb, n_p)
    new_shift_s = z_rows(n_p, n_p + ns)
    unstate = lambda s, nseq: s.reshape(nseq, HEAD_DIM, n_heads, HEAD_DIM).transpose(0, 2, 1, 3)
    return (y_prompt, y_sample.reshape(ns, 1, d), new_conv_p, new_shift_p, unstate(st_p, nb),
            new_conv_s, new_shift_s, unstate(st_s, ns))


def kernel(x_prompt, x_sample, state_conv, state_shift, state_wkv, w_in, mu_shift, conv_w, conv_b, conv_ln_g, conv_ln_b, w0, w_up, a0, a_up, g_up, k_k, k_a, r_k, gn_g, gn_b, w_out, ln1_g, ln1_b, router_w, router_bias, exp_gate, exp_up, exp_down, sh_gate, sh_up, sh_down, ln2_g, ln2_b):
    depth = w_in.shape[0]
    alpha = (2.0 * depth) ** 0.25
    names = ("w_in", "mu_shift", "conv_w", "conv_b", "conv_ln_g", "conv_ln_b", "w0", "w_up", "a0", "a_up", "g_up",
             "k_k", "k_a", "r_k", "gn_g", "gn_b", "w_out", "ln1_g", "ln1_b", "router_w", "router_bias", "exp_gate",
             "exp_up", "exp_down", "sh_gate", "sh_up", "sh_down", "ln2_g", "ln2_b")
    vals = (w_in, mu_shift, conv_w, conv_b, conv_ln_g, conv_ln_b, w0, w_up, a0, a_up, g_up, k_k, k_a, r_k, gn_g,
            gn_b, w_out, ln1_g, ln1_b, router_w, router_bias, exp_gate, exp_up, exp_down, sh_gate, sh_up, sh_down,
            ln2_g, ln2_b)
    xp, xs = x_prompt, x_sample
    outs = [[] for _ in range(6)]
    for l in range(depth):
        lp = {n: v[l] for n, v in zip(names, vals)}
        assert xs.shape[1] == 1
        xp, xs, *states = _layer(xp, xs, state_conv[l], state_shift[l], state_wkv[l], lp, alpha)
        for o, s in zip(outs, states):
            o.append(s)
    return (xp, xs) + tuple(jnp.stack(o) for o in outs)
```
